```python
import math
import jax, jax.numpy as jnp
from jax import lax
import numpy as np

D_MODEL = 1024
BATCH = 16
SEQ = 2048
DEPTH = 2

N_MIXERS = 2
EXPAND = 2
D_INNER = EXPAND * D_MODEL
POOL_WINDOWS = (2, 4, 8, 16)
N_POOL_GROUPS = len(POOL_WINDOWS)
POOL_GROUP_DIM = D_INNER // N_POOL_GROUPS
SB_HEADS = 16
SB_QK_DIM = 64
SB_V_DIM = D_INNER // SB_HEADS
SB_QK_WIDTH = SB_HEADS * SB_QK_DIM
Q_BLOCK = 128
RMS_EPS = 1e-6
N_POOL_LAYERS = (DEPTH + 1) // 2
N_SB_LAYERS = DEPTH // 2

kernel_name = "hybrid_pool_stickbreaking_trunk"


def rms_norm(x, g):
    x32 = x.astype(jnp.float32)
    inv = lax.rsqrt(jnp.mean(x32 * x32, axis=-1, keepdims=True) + RMS_EPS)
    return (x32 * inv * g.astype(jnp.float32)).astype(x.dtype)


def causal_pool_minus_self(u, window):
    S = u.shape[1]
    u32 = u.astype(jnp.float32)
    c = jnp.cumsum(u32, axis=1)
    c_shift = jnp.pad(c, ((0, 0), (window, 0), (0, 0)))[:, :S]
    count = jnp.minimum(jnp.arange(S) + 1, window).astype(jnp.float32)
    mean = (c - c_shift) / count[None, :, None]
    return (mean - u32).astype(u.dtype)


def pooling_mixer(u, w_in, w_group, scale, w_out):
    B, S, _ = u.shape
    proj = u @ w_in
    xb, z = jnp.split(proj, 2, axis=-1)
    xg = xb.reshape(B, S, N_POOL_GROUPS, POOL_GROUP_DIM)
    pooled = jnp.stack(
        [causal_pool_minus_self(xg[:, :, gi], w) for gi, w in enumerate(POOL_WINDOWS)],
        axis=2)
    mixed = jnp.einsum('bsgi,gio->bsgo', pooled, w_group).reshape(B, S, D_INNER)
    y = mixed * scale * jax.nn.silu(z)
    return y @ w_out


def stick_breaking_attention(q, k, v):
    S = q.shape[2]
    scale = 1.0 / math.sqrt(q.shape[-1])
    n_blocks = S // Q_BLOCK
    outs = []
    for bi in range(n_blocks):
        q_lo = bi * Q_BLOCK
        k_len = q_lo + Q_BLOCK
        qb = q[:, :, q_lo:q_lo + Q_BLOCK].astype(jnp.float32)
        kb = k[:, :, :k_len].astype(jnp.float32)
        vb = v[:, :, :k_len].astype(jnp.float32)
        z = jnp.einsum('bhqd,bhkd->bhqk', qb, kb) * scale
        q_pos = q_lo + jnp.arange(Q_BLOCK)
        k_pos = jnp.arange(k_len)
        mask = k_pos[None, :] < q_pos[:, None]
        log_beta = jax.nn.log_sigmoid(z)
        log_om = jnp.where(mask, jax.nn.log_sigmoid(-z), 0.0)
        later = lax.cumsum(log_om, axis=3, reverse=True) - log_om
        a = jnp.where(mask, jnp.exp(log_beta + later), 0.0)
        outs.append(jnp.einsum('bhqk,bhkd->bhqd', a, vb))
    return jnp.concatenate(outs, axis=2).astype(v.dtype)


def stick_breaking_mixer(u, w_in, w_out):
    B, S, _ = u.shape
    proj = u @ w_in
    q, k, v, z = jnp.split(
        proj, [SB_QK_WIDTH, 2 * SB_QK_WIDTH, 2 * SB_QK_WIDTH + D_INNER], axis=-1)
    q = q.reshape(B, S, SB_HEADS, SB_QK_DIM).transpose(0, 2, 1, 3)
    k = k.reshape(B, S, SB_HEADS, SB_QK_DIM).transpose(0, 2, 1, 3)
    v = v.reshape(B, S, SB_HEADS, SB_V_DIM).transpose(0, 2, 1, 3)
    o = stick_breaking_attention(q, k, v)
    o = o.transpose(0, 2, 1, 3).reshape(B, S, D_INNER)
    y = o * jax.nn.silu(z)
    return y @ w_out


def setup_inputs(seed: int = 0) -> dict:
    key = jax.random.key(seed)
    ks = jax.random.split(key, 10)
    f32 = jnp.float32
    x = jax.random.normal(ks[0], (BATCH, SEQ, D_MODEL), f32)
    norm_g = 1.0 + 0.02 * jax.random.normal(ks[1], (DEPTH, D_MODEL), f32)
    pool_w_in = jax.random.normal(ks[2], (N_POOL_LAYERS, D_MODEL, 2 * D_INNER), f32) * D_MODEL ** -0.5
    pool_w = jax.random.normal(ks[3], (N_POOL_LAYERS, N_POOL_GROUPS, POOL_GROUP_DIM, POOL_GROUP_DIM), f32) * POOL_GROUP_DIM ** -0.5
    pool_scale = 1.0 + 0.02 * jax.random.normal(ks[4], (N_POOL_LAYERS, D_INNER), f32)
    pool_w_out = jax.random.normal(ks[5], (N_POOL_LAYERS, D_INNER, D_MODEL), f32) * D_INNER ** -0.5
    sb_w_in = jax.random.normal(ks[6], (N_SB_LAYERS, D_MODEL, 2 * SB_QK_WIDTH + 2 * D_INNER), f32) * D_MODEL ** -0.5
    sb_w_out = jax.random.normal(ks[7], (N_SB_LAYERS, D_INNER, D_MODEL), f32) * D_INNER ** -0.5
    norm_f = 1.0 + 0.02 * jax.random.normal(ks[8], (D_MODEL,), f32)
    return {"x": x, "norm_g": norm_g, "pool_w_in": pool_w_in, "pool_w": pool_w,
            "pool_scale": pool_scale, "pool_w_out": pool_w_out, "sb_w_in": sb_w_in,
            "sb_w_out": sb_w_out, "norm_f": norm_f}


def reference(x, norm_g, pool_w_in, pool_w, pool_scale, pool_w_out, sb_w_in, sb_w_out, norm_f):
    h = x
    for i in range(DEPTH):
        u = rms_norm(h, norm_g[i])
        j = i // N_MIXERS
        if i % N_MIXERS == 0:
            h = h + pooling_mixer(u, pool_w_in[j], pool_w[j], pool_scale[j], pool_w_out[j])
        else:
            h = h + stick_breaking_mixer(u, sb_w_in[j], sb_w_out[j])
    return rms_norm(h, norm_f)
```

```python
import functools
import math

import jax
import jax.numpy as jnp
from jax import lax
from jax.experimental import pallas as pl
from jax.experimental.pallas import tpu as pltpu

RMS_EPS = 1e-6
POOL_WINDOWS = (2, 4, 8, 16)
POOL_HALO = 16
SB_HEADS = 16
SB_QK_DIM = 64
LOG2E = 1.4426950408889634
LN2_INV = LOG2E

SUBLANES = 8
ATT_BLOCK = 256
ATT_ROWS = ATT_BLOCK // SUBLANES
VMEM_LIMIT = 56 * 1024 * 1024

f32 = jnp.float32
bf16 = jnp.bfloat16


def _cparams(n_axes):
    return pltpu.CompilerParams(dimension_semantics=("arbitrary",) * n_axes,
                                vmem_limit_bytes=VMEM_LIMIT)


def _norm_proj_kernel(x_ref, g_ref, w_ref, o_ref, *, chunk, scaled_cols, col_scale):
    x = x_ref[...]
    inv = lax.rsqrt(jnp.mean(x * x, axis=-1, keepdims=True) + RMS_EPS)
    u = (x * inv * g_ref[...]).astype(bf16)
    n = o_ref.shape[1]
    for c in range(n // chunk):
        cols = slice(c * chunk, (c + 1) * chunk)
        r = jnp.dot(u, w_ref[:, cols], preferred_element_type=f32)
        if (c + 1) * chunk <= scaled_cols:
            r = r * col_scale
        o_ref[:, cols] = r.astype(o_ref.dtype)


def _norm_proj(h, g, w, *, tm, chunk, scaled_cols=0, col_scale=1.0):
    n_rows, d = h.shape
    n = w.shape[1]
    assert n_rows % tm == 0 and n % chunk == 0 and scaled_cols % chunk == 0
    kern = functools.partial(_norm_proj_kernel, chunk=chunk, scaled_cols=scaled_cols, col_scale=col_scale)
    return pl.pallas_call(
        kern,
        grid=(n_rows // tm,),
        in_specs=[pl.BlockSpec((tm, d), lambda i: (i, 0)),
                  pl.BlockSpec((1, d), lambda i: (0, 0)),
                  pl.BlockSpec((d, n), lambda i: (0, 0))],
        out_specs=pl.BlockSpec((tm, n), lambda i: (i, 0)),
        out_shape=jax.ShapeDtypeStruct((n_rows, n), bf16),
        compiler_params=_cparams(1),
        name="norm_proj",
    )(h, g, w)


def _pool_kernel(p_ref, halo_ref, wg_ref, sc_ref, y_ref, xs_ref, *, tm, seq, windows, gd):
    di = gd * len(windows)
    tile_in_seq = pl.program_id(0) % (seq // tm)
    xs_ref[POOL_HALO:, :] = p_ref[:, :di].astype(f32)
    xs_ref[:POOL_HALO, :] = jnp.where(tile_in_seq == 0, 0.0, halo_ref[...].astype(f32))
    pos = tile_in_seq * tm + lax.broadcasted_iota(jnp.int32, (tm, 1), 0)
    for g, w in enumerate(windows):
        cols = slice(g * gd, (g + 1) * gd)
        x_self = xs_ref[POOL_HALO:POOL_HALO + tm, cols]
        s = x_self
        for d in range(1, w):
            s = s + xs_ref[POOL_HALO - d:POOL_HALO - d + tm, cols]
        count = jnp.minimum(pos + 1, w).astype(f32)
        pooled = s / count - x_self
        mixed = jnp.dot(pooled.astype(bf16), wg_ref[g], preferred_element_type=f32)
        z = p_ref[:, di + g * gd:di + (g + 1) * gd].astype(f32)
        y = mixed * sc_ref[:, cols] * (z * jax.nn.sigmoid(z))
        y_ref[:, cols] = y.astype(y_ref.dtype)


def _pool_mix(proj, wg, scale, *, tm, seq):
    n_rows, two_di = proj.shape
    di = two_di // 2
    n_groups, gd, _ = wg.shape
    assert n_groups == len(POOL_WINDOWS) and gd * n_groups == di
    assert seq % tm == 0 and tm % POOL_HALO == 0 and max(POOL_WINDOWS) <= POOL_HALO
    halo_blocks = tm // POOL_HALO
    kern = functools.partial(_pool_kernel, tm=tm, seq=seq, windows=POOL_WINDOWS, gd=gd)
    return pl.pallas_call(
        kern,
        grid=(n_rows // tm,),
        in_specs=[pl.BlockSpec((tm, two_di), lambda i: (i, 0)),
                  pl.BlockSpec((POOL_HALO, di), lambda i: (jnp.maximum(i * halo_blocks - 1, 0), 0)),
                  pl.BlockSpec((n_groups, gd, gd), lambda i: (0, 0, 0)),
                  pl.BlockSpec((1, di), lambda i: (0, 0))],
        out_specs=pl.BlockSpec((tm, di), lambda i: (i, 0)),
        out_shape=jax.ShapeDtypeStruct((n_rows, di), bf16),
        scratch_shapes=[pltpu.VMEM((tm + POOL_HALO, di), f32)],
        compiler_params=_cparams(1),
        name="pool_mix",
    )(proj, proj, wg, scale)


def _out_proj_kernel(y_ref, w_ref, h_ref, o_ref):
    o_ref[...] = h_ref[...] + jnp.dot(y_ref[...], w_ref[...], preferred_element_type=f32)


def _out_proj(y, w, h, *, tm):
    n_rows, di = y.shape
    d = w.shape[1]
    return pl.pallas_call(
        _out_proj_kernel,
        grid=(n_rows // tm,),
        in_specs=[pl.BlockSpec((tm, di), lambda i: (i, 0)),
                  pl.BlockSpec((di, d), lambda i: (0, 0)),
                  pl.BlockSpec((tm, d), lambda i: (i, 0))],
        out_specs=pl.BlockSpec((tm, d), lambda i: (i, 0)),
        out_shape=jax.ShapeDtypeStruct((n_rows, d), f32),
        compiler_params=_cparams(1),
        name="out_proj",
    )(y, w, h)


def _gated_out_proj_norm_kernel(o_ref, z_ref, w_ref, h_ref, g_ref, out_ref):
    z = z_ref[...].astype(f32)
    y = (o_ref[...].astype(f32) * (z * jax.nn.sigmoid(z))).astype(bf16)
    h = h_ref[...] + jnp.dot(y, w_ref[...], preferred_element_type=f32)
    inv = lax.rsqrt(jnp.mean(h * h, axis=-1, keepdims=True) + RMS_EPS)
    out_ref[...] = h * inv * g_ref[...]


def _gated_out_proj_norm(o, proj, gate_block, w, h, g, *, tm):
    n_rows, di = o.shape
    d = w.shape[1]
    return pl.pallas_call(
        _gated_out_proj_norm_kernel,
        grid=(n_rows // tm,),
        in_specs=[pl.BlockSpec((tm, di), lambda i: (i, 0)),
                  pl.BlockSpec((tm, di), lambda i: (i, gate_block)),
                  pl.BlockSpec((di, d), lambda i: (0, 0)),
                  pl.BlockSpec((tm, d), lambda i: (i, 0)),
                  pl.BlockSpec((1, d), lambda i: (0, 0))],
        out_specs=pl.BlockSpec((tm, d), lambda i: (i, 0)),
        out_shape=jax.ShapeDtypeStruct((n_rows, d), f32),
        compiler_params=_cparams(1),
        name="gated_out_proj_norm",
    )(o, proj, w, h, g)


def _neg_abs(x):
    return pltpu.bitcast(pltpu.bitcast(x, jnp.uint32) | jnp.uint32(0x80000000), f32)


def _suffix_sum_sublanes(a):
    sub = lax.broadcasted_iota(jnp.int32, a.shape, 0)
    t = a
    for sh in (1, 2, 4):
        rolled = pltpu.roll(t, SUBLANES - sh, axis=0)
        t = t + jnp.where(sub + sh < SUBLANES, rolled, 0.0)
    return t


def _sb_attention_kernel(q_ref, k_ref, v_ref, o_ref, vt_ref, p_ref, acc_ref):
    qi = pl.program_id(2)
    tq = q_ref.shape[0]
    dk2 = q_ref.shape[1]
    dv = v_ref.shape[1] // 2
    n_kb = k_ref.shape[0] // ATT_BLOCK

    @pl.when(qi == 0)
    def _():
        for kb in range(n_kb):
            vb = v_ref[kb * ATT_BLOCK:(kb + 1) * ATT_BLOCK, :].astype(f32)
            vt_ref[kb] = vb.T.astype(bf16)

    q2 = q_ref[...]
    lane = lax.broadcasted_iota(jnp.int32, q2.shape, 1)
    zero = jnp.zeros_like(q2)
    qm = (jnp.where(lane < dk2 // 2, q2, zero), jnp.where(lane >= dk2 // 2, q2, zero))

    c = lax.broadcasted_iota(jnp.int32, (SUBLANES, tq), 1)
    t_pos = (c % SUBLANES) * ATT_ROWS + c // SUBLANES
    s_base = lax.broadcasted_iota(jnp.int32, (SUBLANES, tq), 0) * ATT_ROWS

    def block(kb, h, carry, masked, first):
        k_blk = k_ref[pl.ds(pl.multiple_of(kb * ATT_BLOCK, ATT_BLOCK), ATT_BLOCK), :]
        zt = lax.dot_general(k_blk, qm[h], (((1,), (1,)), ((), ())), preferred_element_type=f32)
        acc = jnp.zeros((SUBLANES, tq), f32)
        for r in reversed(range(ATT_ROWS)):
            x = zt[r * SUBLANES:(r + 1) * SUBLANES, :]
            e = jnp.exp2(_neg_abs(x))
            lb = jnp.minimum(x, 0.0) - jnp.log(1.0 + e) * LN2_INV
            p = lb + acc
            if masked:
                acc = acc + jnp.where(s_base + r < t_pos, lb - x, 0.0)
            else:
                acc = p - x
            p_ref[h, r * SUBLANES:(r + 1) * SUBLANES, :] = p
        suffix = _suffix_sum_sublanes(acc)
        offs = carry + (suffix - acc)
        new_carry = carry + jnp.broadcast_to(suffix[0:1, :], suffix.shape)
        pieces = []
        for r in range(ATT_ROWS):
            a = jnp.exp2(p_ref[h, r * SUBLANES:(r + 1) * SUBLANES, :] + offs)
            if masked:
                a = jnp.where(s_base + r < t_pos, a, 0.0)
            pieces.append(a)
        a_t = jnp.concatenate(pieces, axis=0).astype(bf16)
        o_t = jnp.dot(vt_ref[kb, h * dv:(h + 1) * dv, :], a_t, preferred_element_type=f32)
        if first:
            acc_ref[h] = o_t
        else:
            acc_ref[h] += o_t
        return new_carry

    zero_carry = jnp.zeros((SUBLANES, tq), f32)
    carry0 = tuple(block(qi, h, zero_carry, True, True) for h in range(2))

    def body(j, carry):
        kb = qi - 1 - j
        return tuple(block(kb, h, carry[h], False, False) for h in range(2))

    lax.fori_loop(0, qi, body, carry0)

    for h in range(2):
        o_ref[:, h * dv:(h + 1) * dv] = acc_ref[h].T.astype(o_ref.dtype)


def _sb_attention(proj, *, batch, seq, qk_width, d_inner):
    n_rows = proj.shape[0]
    n_pairs = SB_HEADS // 2
    dk2 = 2 * SB_QK_DIM
    dv2 = 2 * (d_inner // SB_HEADS)
    nq = seq // ATT_BLOCK
    k_col0 = qk_width // dk2
    v_col0 = 2 * qk_width // dv2
    assert seq % ATT_BLOCK == 0 and qk_width == SB_HEADS * SB_QK_DIM and (2 * qk_width) % dv2 == 0
    return pl.pallas_call(
        _sb_attention_kernel,
        grid=(batch, n_pairs, nq),
        in_specs=[pl.BlockSpec((ATT_BLOCK, dk2), lambda b, p, i: (b * nq + i, p)),
                  pl.BlockSpec((seq, dk2), lambda b, p, i: (b, k_col0 + p)),
                  pl.BlockSpec((seq, dv2), lambda b, p, i: (b, v_col0 + p))],
        out_specs=pl.BlockSpec((ATT_BLOCK, dv2), lambda b, p, i: (b * nq + i, p)),
        out_shape=jax.ShapeDtypeStruct((n_rows, d_inner), bf16),
        scratch_shapes=[pltpu.VMEM((nq, dv2, ATT_BLOCK), bf16),
                        pltpu.VMEM((2, ATT_BLOCK, ATT_BLOCK), f32),
                        pltpu.VMEM((2, dv2 // 2, ATT_BLOCK), f32)],
        compiler_params=_cparams(3),
        name="sb_attention",
    )(proj, proj, proj)


def _permute_tokens(a):
    n, f = a.shape
    return a.reshape(n // ATT_BLOCK, SUBLANES, ATT_ROWS, f).swapaxes(1, 2).reshape(n, f)


def _unpermute_tokens(a):
    n, f = a.shape
    return a.reshape(n // ATT_BLOCK, ATT_ROWS, SUBLANES, f).swapaxes(1, 2).reshape(n, f)


def kernel(x, norm_g, pool_w_in, pool_w, pool_scale, pool_w_out, sb_w_in, sb_w_out, norm_f):
    batch, seq, d = x.shape
    depth = norm_g.shape[0]
    d_inner = pool_w_out.shape[1]
    qk_width = (sb_w_in.shape[2] - 2 * d_inner) // 2
    n_rows = batch * seq
    tm = 512 if seq % 512 == 0 else ATT_BLOCK

    h = x.reshape(n_rows, d)
    for i in range(depth):
        j = i // 2
        g = norm_g[i].reshape(1, d)
        last = i == depth - 1
        if i % 2 == 0:
            proj = _norm_proj(h, g, pool_w_in[j].astype(bf16), tm=tm, chunk=512)
            y = _pool_mix(proj, pool_w[j].astype(bf16), pool_scale[j].reshape(1, d_inner), tm=tm, seq=seq)
            h = _out_proj(y, pool_w_out[j].astype(bf16), h, tm=tm)
            if last:
                h = _final_norm(h, norm_f.reshape(1, d), tm=tm)
        else:
            hp = _permute_tokens(h)
            proj = _norm_proj(hp, g, sb_w_in[j].astype(bf16), tm=tm, chunk=512,
                              scaled_cols=qk_width, col_scale=LOG2E / math.sqrt(SB_QK_DIM))
            o = _sb_attention(proj, batch=batch, seq=seq, qk_width=qk_width, d_inner=d_inner)
            gate_block = 2 * qk_width // d_inner + 1
            if last:
                hp = _gated_out_proj_norm(o, proj, gate_block, sb_w_out[j].astype(bf16), hp,
                                          norm_f.reshape(1, d), tm=tm)
            else:
                raise NotImplementedError("stick-breaking layer is only supported as the last layer")
            h = _unpermute_tokens(hp)
    return h.reshape(batch, seq, d)


def _final_norm_kernel(h_ref, g_ref, o_ref):
    h = h_ref[...]
    inv = lax.rsqrt(jnp.mean(h * h, axis=-1, keepdims=True) + RMS_EPS)
    o_ref[...] = h * inv * g_ref[...]


def _final_norm(h, g, *, tm):
    n_rows, d = h.shape
    return pl.pallas_call(
        _final_norm_kernel,
        grid=(n_rows // tm,),
        in_specs=[pl.BlockSpec((tm, d), lambda i: (i, 0)), pl.BlockSpec((1, d), lambda i: (0, 0))],
        out_specs=pl.BlockSpec((tm, d), lambda i: (i, 0)),
        out_shape=jax.ShapeDtypeStruct((n_rows, d), f32),
        compiler_params=_cparams(1),
        name="final_norm",
    )(h, g)
```

```python
import functools
import math

import jax
import jax.numpy as jnp
from jax import lax
from jax.experimental import pallas as pl
from jax.experimental.pallas import tpu as pltpu

RMS_EPS = 1e-6
POOL_WINDOWS = (2, 4, 8, 16)
POOL_HALO = 16
SB_HEADS = 16
SB_QK_DIM = 64
LOG2E = 1.4426950408889634
LN2_INV = LOG2E

SUBLANES = 8
ATT_BLOCK = 256
ATT_ROWS = ATT_BLOCK // SUBLANES
VMEM_LIMIT = 56 * 1024 * 1024

f32 = jnp.float32
bf16 = jnp.bfloat16


def _cparams(n_axes):
    return pltpu.CompilerParams(dimension_semantics=("arbitrary",) * n_axes,
                                vmem_limit_bytes=VMEM_LIMIT)


def _norm_proj_kernel(x_ref, g_ref, w_ref, o_ref, *, chunk, scaled_cols, col_scale):
    x = x_ref[...]
    inv = lax.rsqrt(jnp.mean(x * x, axis=-1, keepdims=True) + RMS_EPS)
    u = (x * inv * g_ref[...]).astype(bf16)
    n = o_ref.shape[1]
    for c in range(n // chunk):
        cols = slice(c * chunk, (c + 1) * chunk)
        r = jnp.dot(u, w_ref[:, cols], preferred_element_type=f32)
        if (c + 1) * chunk <= scaled_cols:
            r = r * col_scale
        o_ref[:, cols] = r.astype(o_ref.dtype)


def _norm_proj(h, g, w, *, tm, chunk, scaled_cols=0, col_scale=1.0):
    n_rows, d = h.shape
    n = w.shape[1]
    assert n_rows % tm == 0 and n % chunk == 0 and scaled_cols % chunk == 0
    kern = functools.partial(_norm_proj_kernel, chunk=chunk, scaled_cols=scaled_cols, col_scale=col_scale)
    return pl.pallas_call(
        kern,
        grid=(n_rows // tm,),
        in_specs=[pl.BlockSpec((tm, d), lambda i: (i, 0)),
                  pl.BlockSpec((1, d), lambda i: (0, 0)),
                  pl.BlockSpec((d, n), lambda i: (0, 0))],
        out_specs=pl.BlockSpec((tm, n), lambda i: (i, 0)),
        out_shape=jax.ShapeDtypeStruct((n_rows, n), bf16),
        compiler_params=_cparams(1),
        name="norm_proj",
    )(h, g, w)


def _pool_kernel(p_ref, halo_ref, wg_ref, sc_ref, y_ref, xs_ref, *, tm, seq, windows, gd):
    di = gd * len(windows)
    tile_in_seq = pl.program_id(0) % (seq // tm)
    xs_ref[POOL_HALO:, :] = p_ref[:, :di].astype(f32)
    xs_ref[:POOL_HALO, :] = jnp.where(tile_in_seq == 0, 0.0, halo_ref[...].astype(f32))
    pos = tile_in_seq * tm + lax.broadcasted_iota(jnp.int32, (tm, 1), 0)
    for g, w in enumerate(windows):
        cols = slice(g * gd, (g + 1) * gd)
        x_self = xs_ref[POOL_HALO:POOL_HALO + tm, cols]
        s = x_self
        for d in range(1, w):
            s = s + xs_ref[POOL_HALO - d:POOL_HALO - d + tm, cols]
        count = jnp.minimum(pos + 1, w).astype(f32)
        pooled = s / count - x_self
        mixed = jnp.dot(pooled.astype(bf16), wg_ref[g], preferred_element_type=f32)
        z = p_ref[:, di + g * gd:di + (g + 1) * gd].astype(f32)
        y = mixed * sc_ref[:, cols] * (z * jax.nn.sigmoid(z))
        y_ref[:, cols] = y.astype(y_ref.dtype)


def _pool_mix(proj, wg, scale, *, tm, seq):
    n_rows, two_di = proj.shape
    di = two_di // 2
    n_groups, gd, _ = wg.shape
    assert n_groups == len(POOL_WINDOWS) and gd * n_groups == di
    assert seq % tm == 0 and tm % POOL_HALO == 0 and max(POOL_WINDOWS) <= POOL_HALO
    halo_blocks = tm // POOL_HALO
    kern = functools.partial(_pool_kernel, tm=tm, seq=seq, windows=POOL_WINDOWS, gd=gd)
    return pl.pallas_call(
        kern,
        grid=(n_rows // tm,),
        in_specs=[pl.BlockSpec((tm, two_di), lambda i: (i, 0)),
                  pl.BlockSpec((POOL_HALO, di), lambda i: (jnp.maximum(i * halo_blocks - 1, 0), 0)),
                  pl.BlockSpec((n_groups, gd, gd), lambda i: (0, 0, 0)),
                  pl.BlockSpec((1, di), lambda i: (0, 0))],
        out_specs=pl.BlockSpec((tm, di), lambda i: (i, 0)),
        out_shape=jax.ShapeDtypeStruct((n_rows, di), bf16),
        scratch_shapes=[pltpu.VMEM((tm + POOL_HALO, di), f32)],
        compiler_params=_cparams(1),
        name="pool_mix",
    )(proj, proj, wg, scale)


def _out_proj_kernel(y_ref, w_ref, h_ref, o_ref):
    o_ref[...] = h_ref[...] + jnp.dot(y_ref[...], w_ref[...], preferred_element_type=f32)


def _out_proj(y, w, h, *, tm):
    n_rows, di = y.shape
    d = w.shape[1]
    return pl.pallas_call(
        _out_proj_kernel,
        grid=(n_rows // tm,),
        in_specs=[pl.BlockSpec((tm, di), lambda i: (i, 0)),
                  pl.BlockSpec((di, d), lambda i: (0, 0)),
                  pl.BlockSpec((tm, d), lambda i: (i, 0))],
        out_specs=pl.BlockSpec((tm, d), lambda i: (i, 0)),
        out_shape=jax.ShapeDtypeStruct((n_rows, d), f32),
        compiler_params=_cparams(1),
        name="out_proj",
    )(y, w, h)


def _gated_out_proj_norm_kernel(o_ref, z_ref, w_ref, h_ref, g_ref, out_ref):
    z = z_ref[...].astype(f32)
    y = (o_ref[...].astype(f32) * (z * jax.nn.sigmoid(z))).astype(bf16)
    h = h_ref[...] + jnp.dot(y, w_ref[...], preferred_element_type=f32)
    inv = lax.rsqrt(jnp.mean(h * h, axis=-1, keepdims=True) + RMS_EPS)
    out_ref[...] = h * inv * g_ref[...]


def _gated_out_proj_norm(o, proj, gate_block, w, h, g, *, tm):
    n_rows, di = o.shape
    d = w.shape[1]
    return pl.pallas_call(
        _gated_out_proj_norm_kernel,
        grid=(n_rows // tm,),
        in_specs=[pl.BlockSpec((tm, di), lambda i: (i, 0)),
                  pl.BlockSpec((tm, di), lambda i: (i, gate_block)),
                  pl.BlockSpec((di, d), lambda i: (0, 0)),
                  pl.BlockSpec((tm, d), lambda i: (i, 0)),
                  pl.BlockSpec((1, d), lambda i: (0, 0))],
        out_specs=pl.BlockSpec((tm, d), lambda i: (i, 0)),
        out_shape=jax.ShapeDtypeStruct((n_rows, d), f32),
        compiler_params=_cparams(1),
        name="gated_out_proj_norm",
    )(o, proj, w, h, g)


def _suffix_sum_sublanes(a):
    sub = lax.broadcasted_iota(jnp.int32, a.shape, 0)
    t = a
    for sh in (1, 2, 4):
        rolled = pltpu.roll(t, SUBLANES - sh, axis=0)
        t = t + jnp.where(sub + sh < SUBLANES, rolled, 0.0)
    return t


def _sb_attention_kernel(q_ref, k_ref, v_ref, o_ref, vt_ref, z_ref, p_ref, at_ref, acc_ref):
    qi = pl.program_id(2)
    tq = q_ref.shape[0]
    dk2 = q_ref.shape[1]
    dv = v_ref.shape[1] // 2
    n_kb = k_ref.shape[0] // ATT_BLOCK
    heads = range(2)

    @pl.when(qi == 0)
    def _():
        for kb in range(n_kb):
            vb = v_ref[kb * ATT_BLOCK:(kb + 1) * ATT_BLOCK, :].astype(f32)
            vt_ref[kb] = vb.T.astype(bf16)

    q2 = q_ref[...]
    lane = lax.broadcasted_iota(jnp.int32, q2.shape, 1)
    zero = jnp.zeros_like(q2)
    qm = (jnp.where(lane < dk2 // 2, q2, zero), jnp.where(lane >= dk2 // 2, q2, zero))

    c = lax.broadcasted_iota(jnp.int32, (SUBLANES, tq), 1)
    t_pos = (c % SUBLANES) * ATT_ROWS + c // SUBLANES
    s_base = lax.broadcasted_iota(jnp.int32, (SUBLANES, tq), 0) * ATT_ROWS

    def scores(kb, h):
        k_blk = k_ref[pl.ds(pl.multiple_of(kb * ATT_BLOCK, ATT_BLOCK), ATT_BLOCK), :]
        z_ref[h] = lax.dot_general(k_blk, qm[h], (((1,), (1,)), ((), ())), preferred_element_type=f32)

    def pass1(h, masked):
        acc = jnp.zeros((SUBLANES, tq), f32)
        for r in reversed(range(ATT_ROWS)):
            rows = slice(r * SUBLANES, (r + 1) * SUBLANES)
            x = z_ref[h, rows, :]
            e = jnp.exp2(-jnp.abs(x))
            lb = jnp.minimum(x, 0.0) - jnp.log(1.0 + e) * LN2_INV
            p = lb + acc
            if masked:
                acc = acc + jnp.where(s_base + r < t_pos, lb - x, 0.0)
            else:
                acc = p - x
            p_ref[h, rows, :] = p
        return acc

    def pass2(h, acc, carry, masked):
        suffix = _suffix_sum_sublanes(acc)
        offs = carry + (suffix - acc)
        pieces = []
        for r in range(ATT_ROWS):
            a = jnp.exp2(p_ref[h, r * SUBLANES:(r + 1) * SUBLANES, :] + offs)
            if masked:
                a = jnp.where(s_base + r < t_pos, a, 0.0)
            pieces.append(a)
        at_ref[h] = jnp.concatenate(pieces, axis=0).astype(bf16)
        return carry + jnp.broadcast_to(suffix[0:1, :], suffix.shape)

    def weighted_sum(kb, h):
        acc_ref[h] += jnp.dot(vt_ref[kb, h * dv:(h + 1) * dv, :], at_ref[h], preferred_element_type=f32)

    def step(kb, carry, masked, pending):
        if pending:
            for h in heads:
                weighted_sum(kb + 1, h)
        kb_next = jnp.maximum(kb - 1, 0)
        new_carry = []
        for h in heads:
            acc = pass1(h, masked)
            scores(kb_next, h)
            new_carry.append(pass2(h, acc, carry[h], masked))
        return tuple(new_carry)

    acc_ref[...] = jnp.zeros_like(acc_ref)
    for h in heads:
        scores(qi, h)
    zero_carry = jnp.zeros((SUBLANES, tq), f32)
    carry = step(qi, (zero_carry, zero_carry), True, False)
    lax.fori_loop(0, qi, lambda j, cr: step(qi - 1 - j, cr, False, True), carry)
    for h in heads:
        weighted_sum(0, h)
        o_ref[:, h * dv:(h + 1) * dv] = acc_ref[h].T.astype(o_ref.dtype)


def _sb_attention(proj, *, batch, seq, qk_width, d_inner):
    n_rows = proj.shape[0]
    n_pairs = SB_HEADS // 2
    dk2 = 2 * SB_QK_DIM
    dv2 = 2 * (d_inner // SB_HEADS)
    nq = seq // ATT_BLOCK
    k_col0 = qk_width // dk2
    v_col0 = 2 * qk_width // dv2
    assert seq % ATT_BLOCK == 0 and qk_width == SB_HEADS * SB_QK_DIM and (2 * qk_width) % dv2 == 0
    return pl.pallas_call(
        _sb_attention_kernel,
        grid=(batch, n_pairs, nq),
        in_specs=[pl.BlockSpec((ATT_BLOCK, dk2), lambda b, p, i: (b * nq + i, p)),
                  pl.BlockSpec((seq, dk2), lambda b, p, i: (b, k_col0 + p)),
                  pl.BlockSpec((seq, dv2), lambda b, p, i: (b, v_col0 + p))],
        out_specs=pl.BlockSpec((ATT_BLOCK, dv2), lambda b, p, i: (b * nq + i, p)),
        out_shape=jax.ShapeDtypeStruct((n_rows, d_inner), bf16),
        scratch_shapes=[pltpu.VMEM((nq, dv2, ATT_BLOCK), bf16),
                        pltpu.VMEM((2, ATT_BLOCK, ATT_BLOCK), f32),
                        pltpu.VMEM((2, ATT_BLOCK, ATT_BLOCK), f32),
                        pltpu.VMEM((2, ATT_BLOCK, ATT_BLOCK), bf16),
                        pltpu.VMEM((2, dv2 // 2, ATT_BLOCK), f32)],
        compiler_params=_cparams(3),
        name="sb_attention",
    )(proj, proj, proj)


def _permute_tokens(a):
    n, f = a.shape
    return a.reshape(n // ATT_BLOCK, SUBLANES, ATT_ROWS, f).swapaxes(1, 2).reshape(n, f)


def _unpermute_tokens(a):
    n, f = a.shape
    return a.reshape(n // ATT_BLOCK, ATT_ROWS, SUBLANES, f).swapaxes(1, 2).reshape(n, f)


def kernel(x, norm_g, pool_w_in, pool_w, pool_scale, pool_w_out, sb_w_in, sb_w_out, norm_f):
    batch, seq, d = x.shape
    depth = norm_g.shape[0]
    d_inner = pool_w_out.shape[1]
    qk_width = (sb_w_in.shape[2] - 2 * d_inner) // 2
    n_rows = batch * seq
    tm = 512 if seq % 512 == 0 else ATT_BLOCK

    h = x.reshape(n_rows, d)
    for i in range(depth):
        j = i // 2
        g = norm_g[i].reshape(1, d)
        last = i == depth - 1
        if i % 2 == 0:
            proj = _norm_proj(h, g, pool_w_in[j].astype(bf16), tm=tm, chunk=512)
            y = _pool_mix(proj, pool_w[j].astype(bf16), pool_scale[j].reshape(1, d_inner), tm=tm, seq=seq)
            h = _out_proj(y, pool_w_out[j].astype(bf16), h, tm=tm)
            if last:
                h = _final_norm(h, norm_f.reshape(1, d), tm=tm)
        else:
            hp = _permute_tokens(h)
            proj = _norm_proj(hp, g, sb_w_in[j].astype(bf16), tm=tm, chunk=512,
                              scaled_cols=qk_width, col_scale=LOG2E / math.sqrt(SB_QK_DIM))
            o = _sb_attention(proj, batch=batch, seq=seq, qk_width=qk_width, d_inner=d_inner)
            gate_block = 2 * qk_width // d_inner + 1
            if last:
                hp = _gated_out_proj_norm(o, proj, gate_block, sb_w_out[j].astype(bf16), hp,
                                          norm_f.reshape(1, d), tm=tm)
            else:
                raise NotImplementedError("stick-breaking layer is only supported as the last layer")
            h = _unpermute_tokens(hp)
    return h.reshape(batch, seq, d)


def _final_norm_kernel(h_ref, g_ref, o_ref):
    h = h_ref[...]
    inv = lax.rsqrt(jnp.mean(h * h, axis=-1, keepdims=True) + RMS_EPS)
    o_ref[...] = h * inv * g_ref[...]


def _final_norm(h, g, *, tm):
    n_rows, d = h.shape
    return pl.pallas_call(
        _final_norm_kernel,
        grid=(n_rows // tm,),
        in_specs=[pl.BlockSpec((tm, d), lambda i: (i, 0)), pl.BlockSpec((1, d), lambda i: (0, 0))],
        out_specs=pl.BlockSpec((tm, d), lambda i: (i, 0)),
        out_shape=jax.ShapeDtypeStruct((n_rows, d), f32),
        compiler_params=_cparams(1),
        name="final_norm",
    )(h, g)
```

```python
import functools
import math

import jax
import jax.numpy as jnp
from jax import lax
from jax.experimental import pallas as pl
from jax.experimental.pallas import tpu as pltpu

RMS_EPS = 1e-6
POOL_WINDOWS = (2, 4, 8, 16)
POOL_HALO = 16
SB_HEADS = 16
SB_QK_DIM = 64
LOG2E = 1.4426950408889634
LN2_INV = LOG2E
MASKED_SCORE = -1e30

SUBLANES = 8
ATT_BLOCK = 256
ATT_ROWS = ATT_BLOCK // SUBLANES
VMEM_LIMIT = 56 * 1024 * 1024

f32 = jnp.float32
bf16 = jnp.bfloat16


def _cparams(n_axes):
    return pltpu.CompilerParams(dimension_semantics=("arbitrary",) * n_axes,
                                vmem_limit_bytes=VMEM_LIMIT)


def _norm_proj_kernel(x_ref, g_ref, w_ref, o_ref, *, chunk, scaled_cols, col_scale):
    x = x_ref[...]
    inv = lax.rsqrt(jnp.mean(x * x, axis=-1, keepdims=True) + RMS_EPS)
    u = (x * inv * g_ref[...]).astype(bf16)
    n = o_ref.shape[1]
    for c in range(n // chunk):
        cols = slice(c * chunk, (c + 1) * chunk)
        r = jnp.dot(u, w_ref[:, cols], preferred_element_type=f32)
        if (c + 1) * chunk <= scaled_cols:
            r = r * col_scale
        o_ref[:, cols] = r.astype(o_ref.dtype)


def _norm_proj(h, g, w, *, tm, chunk, scaled_cols=0, col_scale=1.0):
    n_rows, d = h.shape
    n = w.shape[1]
    assert n_rows % tm == 0 and n % chunk == 0 and scaled_cols % chunk == 0
    kern = functools.partial(_norm_proj_kernel, chunk=chunk, scaled_cols=scaled_cols, col_scale=col_scale)
    return pl.pallas_call(
        kern,
        grid=(n_rows // tm,),
        in_specs=[pl.BlockSpec((tm, d), lambda i: (i, 0)),
                  pl.BlockSpec((1, d), lambda i: (0, 0)),
                  pl.BlockSpec((d, n), lambda i: (0, 0))],
        out_specs=pl.BlockSpec((tm, n), lambda i: (i, 0)),
        out_shape=jax.ShapeDtypeStruct((n_rows, n), bf16),
        compiler_params=_cparams(1),
        name="norm_proj",
    )(h, g, w)


def _pool_kernel(p_ref, halo_ref, wg_ref, sc_ref, y_ref, xs_ref, *, tm, seq, windows, gd):
    di = gd * len(windows)
    tile_in_seq = pl.program_id(0) % (seq // tm)
    xs_ref[POOL_HALO:, :] = p_ref[:, :di].astype(f32)
    xs_ref[:POOL_HALO, :] = jnp.where(tile_in_seq == 0, 0.0, halo_ref[...].astype(f32))
    pos = tile_in_seq * tm + lax.broadcasted_iota(jnp.int32, (tm, 1), 0)
    for g, w in enumerate(windows):
        cols = slice(g * gd, (g + 1) * gd)
        x_self = xs_ref[POOL_HALO:POOL_HALO + tm, cols]
        s = x_self
        for d in range(1, w):
            s = s + xs_ref[POOL_HALO - d:POOL_HALO - d + tm, cols]
        count = jnp.minimum(pos + 1, w).astype(f32)
        pooled = s / count - x_self
        mixed = jnp.dot(pooled.astype(bf16), wg_ref[g], preferred_element_type=f32)
        z = p_ref[:, di + g * gd:di + (g + 1) * gd].astype(f32)
        y = mixed * sc_ref[:, cols] * (z * jax.nn.sigmoid(z))
        y_ref[:, cols] = y.astype(y_ref.dtype)


def _pool_mix(proj, wg, scale, *, tm, seq):
    n_rows, two_di = proj.shape
    di = two_di // 2
    n_groups, gd, _ = wg.shape
    assert n_groups == len(POOL_WINDOWS) and gd * n_groups == di
    assert seq % tm == 0 and tm % POOL_HALO == 0 and max(POOL_WINDOWS) <= POOL_HALO
    halo_blocks = tm // POOL_HALO
    kern = functools.partial(_pool_kernel, tm=tm, seq=seq, windows=POOL_WINDOWS, gd=gd)
    return pl.pallas_call(
        kern,
        grid=(n_rows // tm,),
        in_specs=[pl.BlockSpec((tm, two_di), lambda i: (i, 0)),
                  pl.BlockSpec((POOL_HALO, di), lambda i: (jnp.maximum(i * halo_blocks - 1, 0), 0)),
                  pl.BlockSpec((n_groups, gd, gd), lambda i: (0, 0, 0)),
                  pl.BlockSpec((1, di), lambda i: (0, 0))],
        out_specs=pl.BlockSpec((tm, di), lambda i: (i, 0)),
        out_shape=jax.ShapeDtypeStruct((n_rows, di), bf16),
        scratch_shapes=[pltpu.VMEM((tm + POOL_HALO, di), f32)],
        compiler_params=_cparams(1),
        name="pool_mix",
    )(proj, proj, wg, scale)


def _out_proj_kernel(y_ref, w_ref, h_ref, o_ref):
    o_ref[...] = h_ref[...] + jnp.dot(y_ref[...], w_ref[...], preferred_element_type=f32)


def _out_proj(y, w, h, *, tm):
    n_rows, di = y.shape
    d = w.shape[1]
    return pl.pallas_call(
        _out_proj_kernel,
        grid=(n_rows // tm,),
        in_specs=[pl.BlockSpec((tm, di), lambda i: (i, 0)),
                  pl.BlockSpec((di, d), lambda i: (0, 0)),
                  pl.BlockSpec((tm, d), lambda i: (i, 0))],
        out_specs=pl.BlockSpec((tm, d), lambda i: (i, 0)),
        out_shape=jax.ShapeDtypeStruct((n_rows, d), f32),
        compiler_params=_cparams(1),
        name="out_proj",
    )(y, w, h)


def _gated_out_proj_norm_kernel(o_ref, z_ref, w_ref, h_ref, g_ref, out_ref):
    z = z_ref[...].astype(f32)
    y = (o_ref[...].astype(f32) * (z * jax.nn.sigmoid(z))).astype(bf16)
    h = h_ref[...] + jnp.dot(y, w_ref[...], preferred_element_type=f32)
    inv = lax.rsqrt(jnp.mean(h * h, axis=-1, keepdims=True) + RMS_EPS)
    out_ref[...] = h * inv * g_ref[...]


def _gated_out_proj_norm(o, proj, gate_block, w, h, g, *, tm):
    n_rows, di = o.shape
    d = w.shape[1]
    return pl.pallas_call(
        _gated_out_proj_norm_kernel,
        grid=(n_rows // tm,),
        in_specs=[pl.BlockSpec((tm, di), lambda i: (i, 0)),
                  pl.BlockSpec((tm, di), lambda i: (i, gate_block)),
                  pl.BlockSpec((di, d), lambda i: (0, 0)),
                  pl.BlockSpec((tm, d), lambda i: (i, 0)),
                  pl.BlockSpec((1, d), lambda i: (0, 0))],
        out_specs=pl.BlockSpec((tm, d), lambda i: (i, 0)),
        out_shape=jax.ShapeDtypeStruct((n_rows, d), f32),
        compiler_params=_cparams(1),
        name="gated_out_proj_norm",
    )(o, proj, w, h, g)


def _suffix_sum_sublanes(a):
    sub = lax.broadcasted_iota(jnp.int32, a.shape, 0)
    t = a
    for sh in (1, 2, 4):
        rolled = pltpu.roll(t, SUBLANES - sh, axis=0)
        t = t + jnp.where(sub + sh < SUBLANES, rolled, 0.0)
    return t


def _sb_attention_kernel(q_ref, k_ref, v_ref, o_ref, vt_ref, z_ref, p_ref, at_ref, acc_ref):
    tq = ATT_BLOCK
    dk2 = q_ref.shape[1]
    dv = v_ref.shape[1] // 2
    n_blocks = k_ref.shape[0] // ATT_BLOCK
    heads = range(2)

    for kb in range(n_blocks):
        vb = v_ref[kb * ATT_BLOCK:(kb + 1) * ATT_BLOCK, :].astype(f32)
        vt_ref[kb] = vb.T.astype(bf16)

    lane = lax.broadcasted_iota(jnp.int32, (tq, dk2), 1)
    c = lax.broadcasted_iota(jnp.int32, (SUBLANES, tq), 1)
    i = lax.broadcasted_iota(jnp.int32, (SUBLANES, tq), 0)
    diag_margin = (c % SUBLANES) * ATT_ROWS + c // SUBLANES - i * ATT_ROWS

    def scores(qm, kb, h):
        k_blk = k_ref[pl.ds(pl.multiple_of(kb * ATT_BLOCK, ATT_BLOCK), ATT_BLOCK), :]
        z_ref[h] = lax.dot_general(k_blk, qm[h], (((1,), (1,)), ((), ())), preferred_element_type=f32)

    def pass1(h, masked):
        acc = jnp.zeros((SUBLANES, tq), f32)
        for r in reversed(range(ATT_ROWS)):
            rows = slice(r * SUBLANES, (r + 1) * SUBLANES)
            x = z_ref[h, rows, :]
            if masked:
                x = jnp.where(diag_margin > r, x, MASKED_SCORE)
            e = jnp.exp2(-jnp.abs(x))
            lb = jnp.minimum(x, 0.0) - jnp.log(1.0 + e) * LN2_INV
            p = lb + acc
            acc = acc + (lb - x) if masked else p - x
            p_ref[h, rows, :] = p
        return acc

    def pass2(h, acc, carry):
        suffix = _suffix_sum_sublanes(acc)
        offs = carry + (suffix - acc)
        pieces = [jnp.exp2(p_ref[h, r * SUBLANES:(r + 1) * SUBLANES, :] + offs) for r in range(ATT_ROWS)]
        at_ref[h] = jnp.concatenate(pieces, axis=0).astype(bf16)
        return carry + jnp.broadcast_to(suffix[0:1, :], suffix.shape)

    def weighted_sum(slot, kb, h):
        acc_ref[slot, h] += jnp.dot(vt_ref[kb, h * dv:(h + 1) * dv, :], at_ref[h],
                                    preferred_element_type=f32)

    def step(qm, slot, kb, carry, masked, pending):
        if pending:
            for h in heads:
                weighted_sum(slot, kb + 1, h)
        kb_next = jnp.maximum(kb - 1, 0)
        new_carry = []
        for h in heads:
            acc = pass1(h, masked)
            scores(qm, kb_next, h)
            new_carry.append(pass2(h, acc, carry[h]))
        return tuple(new_carry)

    zero_carry = jnp.zeros((SUBLANES, tq), f32)
    for qi in range(n_blocks):
        slot = qi % 2
        rows = slice(qi * tq, (qi + 1) * tq)
        q2 = q_ref[rows, :]
        zero = jnp.zeros_like(q2)
        qm = (jnp.where(lane < dk2 // 2, q2, zero), jnp.where(lane >= dk2 // 2, q2, zero))
        acc_ref[slot] = jnp.zeros(acc_ref.shape[1:], f32)
        for h in heads:
            scores(qm, qi, h)
        carry = step(qm, slot, qi, (zero_carry, zero_carry), True, False)
        if qi > 0:
            lax.fori_loop(0, qi, lambda j, cr: step(qm, slot, qi - 1 - j, cr, False, True), carry)
        for h in heads:
            weighted_sum(slot, 0, h)
            o_ref[rows, h * dv:(h + 1) * dv] = acc_ref[slot, h].T.astype(o_ref.dtype)


def _sb_attention(proj, *, batch, seq, qk_width, d_inner):
    n_rows = proj.shape[0]
    n_pairs = SB_HEADS // 2
    dk2 = 2 * SB_QK_DIM
    dv2 = 2 * (d_inner // SB_HEADS)
    k_col0 = qk_width // dk2
    v_col0 = 2 * qk_width // dv2
    assert seq % ATT_BLOCK == 0 and qk_width == SB_HEADS * SB_QK_DIM and (2 * qk_width) % dv2 == 0
    return pl.pallas_call(
        _sb_attention_kernel,
        grid=(batch, n_pairs),
        in_specs=[pl.BlockSpec((seq, dk2), lambda b, p: (b, p)),
                  pl.BlockSpec((seq, dk2), lambda b, p: (b, k_col0 + p)),
                  pl.BlockSpec((seq, dv2), lambda b, p: (b, v_col0 + p))],
        out_specs=pl.BlockSpec((seq, dv2), lambda b, p: (b, p)),
        out_shape=jax.ShapeDtypeStruct((n_rows, d_inner), bf16),
        scratch_shapes=[pltpu.VMEM((seq // ATT_BLOCK, dv2, ATT_BLOCK), bf16),
                        pltpu.VMEM((2, ATT_BLOCK, ATT_BLOCK), f32),
                        pltpu.VMEM((2, ATT_BLOCK, ATT_BLOCK), f32),
                        pltpu.VMEM((2, ATT_BLOCK, ATT_BLOCK), bf16),
                        pltpu.VMEM((2, 2, dv2 // 2, ATT_BLOCK), f32)],
        compiler_params=_cparams(2),
        name="sb_attention",
    )(proj, proj, proj)


def _permute_tokens(a):
    n, f = a.shape
    return a.reshape(n // ATT_BLOCK, SUBLANES, ATT_ROWS, f).swapaxes(1, 2).reshape(n, f)


def _unpermute_tokens(a):
    n, f = a.shape
    return a.reshape(n // ATT_BLOCK, ATT_ROWS, SUBLANES, f).swapaxes(1, 2).reshape(n, f)


def kernel(x, norm_g, pool_w_in, pool_w, pool_scale, pool_w_out, sb_w_in, sb_w_out, norm_f):
    batch, seq, d = x.shape
    assert norm_g.shape[0] == 2 and pool_w_in.shape[0] == 1 and sb_w_in.shape[0] == 1
    d_inner = pool_w_out.shape[1]
    qk_width = (sb_w_in.shape[2] - 2 * d_inner) // 2
    n_rows = batch * seq
    tm = 512 if seq % 512 == 0 else ATT_BLOCK

    h = x.reshape(n_rows, d)
    proj = _norm_proj(h, norm_g[0].reshape(1, d), pool_w_in[0].astype(bf16), tm=tm, chunk=512)
    y = _pool_mix(proj, pool_w[0].astype(bf16), pool_scale[0].reshape(1, d_inner), tm=tm, seq=seq)
    h = _out_proj(y, pool_w_out[0].astype(bf16), h, tm=tm)

    hp = _permute_tokens(h)
    proj = _norm_proj(hp, norm_g[1].reshape(1, d), sb_w_in[0].astype(bf16), tm=tm, chunk=512,
                      scaled_cols=qk_width, col_scale=LOG2E / math.sqrt(SB_QK_DIM))
    o = _sb_attention(proj, batch=batch, seq=seq, qk_width=qk_width, d_inner=d_inner)
    gate_block = 2 * qk_width // d_inner + 1
    out = _gated_out_proj_norm(o, proj, gate_block, sb_w_out[0].astype(bf16), hp,
                               norm_f.reshape(1, d), tm=tm)
    return _unpermute_tokens(out).reshape(batch, seq, d)
```

```python
import functools
import math

import jax
import jax.numpy as jnp
from jax import lax
from jax.experimental import pallas as pl
from jax.experimental.pallas import tpu as pltpu

RMS_EPS = 1e-6
POOL_WINDOWS = (2, 4, 8, 16)
POOL_HALO = 16
SB_HEADS = 16
SB_QK_DIM = 64
LOG2E = 1.4426950408889634
LN2_INV = LOG2E
MASKED_SCORE = -1e30
UNDERFLOW_LOG2 = 160.0

SUBLANES = 8
ATT_BLOCK = 256
ATT_ROWS = ATT_BLOCK // SUBLANES
VMEM_LIMIT = 56 * 1024 * 1024

f32 = jnp.float32
bf16 = jnp.bfloat16


def _cparams(n_axes):
    return pltpu.CompilerParams(dimension_semantics=("arbitrary",) * n_axes,
                                vmem_limit_bytes=VMEM_LIMIT)


def _norm_proj_kernel(x_ref, g_ref, w_ref, o_ref, *, chunk, scaled_cols, col_scale):
    x = x_ref[...]
    inv = lax.rsqrt(jnp.mean(x * x, axis=-1, keepdims=True) + RMS_EPS)
    u = (x * inv * g_ref[...]).astype(bf16)
    n = o_ref.shape[1]
    for c in range(n // chunk):
        cols = slice(c * chunk, (c + 1) * chunk)
        r = jnp.dot(u, w_ref[:, cols], preferred_element_type=f32)
        if (c + 1) * chunk <= scaled_cols:
            r = r * col_scale
        o_ref[:, cols] = r.astype(o_ref.dtype)


def _norm_proj(h, g, w, *, tm, chunk, scaled_cols=0, col_scale=1.0):
    n_rows, d = h.shape
    n = w.shape[1]
    assert n_rows % tm == 0 and n % chunk == 0 and scaled_cols % chunk == 0
    kern = functools.partial(_norm_proj_kernel, chunk=chunk, scaled_cols=scaled_cols, col_scale=col_scale)
    return pl.pallas_call(
        kern,
        grid=(n_rows // tm,),
        in_specs=[pl.BlockSpec((tm, d), lambda i: (i, 0)),
                  pl.BlockSpec((1, d), lambda i: (0, 0)),
                  pl.BlockSpec((d, n), lambda i: (0, 0))],
        out_specs=pl.BlockSpec((tm, n), lambda i: (i, 0)),
        out_shape=jax.ShapeDtypeStruct((n_rows, n), bf16),
        compiler_params=_cparams(1),
        name="norm_proj",
    )(h, g, w)


def _pool_kernel(p_ref, halo_ref, wg_ref, sc_ref, y_ref, xs_ref, *, tm, seq, windows, gd):
    di = gd * len(windows)
    tile_in_seq = pl.program_id(0) % (seq // tm)
    xs_ref[POOL_HALO:, :] = p_ref[:, :di].astype(f32)
    xs_ref[:POOL_HALO, :] = jnp.where(tile_in_seq == 0, 0.0, halo_ref[...].astype(f32))
    pos = tile_in_seq * tm + lax.broadcasted_iota(jnp.int32, (tm, 1), 0)
    for g, w in enumerate(windows):
        cols = slice(g * gd, (g + 1) * gd)
        x_self = xs_ref[POOL_HALO:POOL_HALO + tm, cols]
        s = x_self
        for d in range(1, w):
            s = s + xs_ref[POOL_HALO - d:POOL_HALO - d + tm, cols]
        count = jnp.minimum(pos + 1, w).astype(f32)
        pooled = s / count - x_self
        mixed = jnp.dot(pooled.astype(bf16), wg_ref[g], preferred_element_type=f32)
        z = p_ref[:, di + g * gd:di + (g + 1) * gd].astype(f32)
        y = mixed * sc_ref[:, cols] * (z * jax.nn.sigmoid(z))
        y_ref[:, cols] = y.astype(y_ref.dtype)


def _pool_mix(proj, wg, scale, *, tm, seq):
    n_rows, two_di = proj.shape
    di = two_di // 2
    n_groups, gd, _ = wg.shape
    assert n_groups == len(POOL_WINDOWS) and gd * n_groups == di
    assert seq % tm == 0 and tm % POOL_HALO == 0 and max(POOL_WINDOWS) <= POOL_HALO
    halo_blocks = tm // POOL_HALO
    kern = functools.partial(_pool_kernel, tm=tm, seq=seq, windows=POOL_WINDOWS, gd=gd)
    return pl.pallas_call(
        kern,
        grid=(n_rows // tm,),
        in_specs=[pl.BlockSpec((tm, two_di), lambda i: (i, 0)),
                  pl.BlockSpec((POOL_HALO, di), lambda i: (jnp.maximum(i * halo_blocks - 1, 0), 0)),
                  pl.BlockSpec((n_groups, gd, gd), lambda i: (0, 0, 0)),
                  pl.BlockSpec((1, di), lambda i: (0, 0))],
        out_specs=pl.BlockSpec((tm, di), lambda i: (i, 0)),
        out_shape=jax.ShapeDtypeStruct((n_rows, di), bf16),
        scratch_shapes=[pltpu.VMEM((tm + POOL_HALO, di), f32)],
        compiler_params=_cparams(1),
        name="pool_mix",
    )(proj, proj, wg, scale)


def _out_proj_kernel(y_ref, w_ref, h_ref, o_ref):
    o_ref[...] = h_ref[...] + jnp.dot(y_ref[...], w_ref[...], preferred_element_type=f32)


def _out_proj(y, w, h, *, tm):
    n_rows, di = y.shape
    d = w.shape[1]
    return pl.pallas_call(
        _out_proj_kernel,
        grid=(n_rows // tm,),
        in_specs=[pl.BlockSpec((tm, di), lambda i: (i, 0)),
                  pl.BlockSpec((di, d), lambda i: (0, 0)),
                  pl.BlockSpec((tm, d), lambda i: (i, 0))],
        out_specs=pl.BlockSpec((tm, d), lambda i: (i, 0)),
        out_shape=jax.ShapeDtypeStruct((n_rows, d), f32),
        compiler_params=_cparams(1),
        name="out_proj",
    )(y, w, h)


def _gated_out_proj_norm_kernel(o_ref, z_ref, w_ref, h_ref, g_ref, out_ref):
    z = z_ref[...].astype(f32)
    y = (o_ref[...].astype(f32) * (z * jax.nn.sigmoid(z))).astype(bf16)
    h = h_ref[...] + jnp.dot(y, w_ref[...], preferred_element_type=f32)
    inv = lax.rsqrt(jnp.mean(h * h, axis=-1, keepdims=True) + RMS_EPS)
    out_ref[...] = h * inv * g_ref[...]


def _gated_out_proj_norm(o, proj, gate_block, w, h, g, *, tm):
    n_rows, di = o.shape
    d = w.shape[1]
    return pl.pallas_call(
        _gated_out_proj_norm_kernel,
        grid=(n_rows // tm,),
        in_specs=[pl.BlockSpec((tm, di), lambda i: (i, 0)),
                  pl.BlockSpec((tm, di), lambda i: (i, gate_block)),
                  pl.BlockSpec((di, d), lambda i: (0, 0)),
                  pl.BlockSpec((tm, d), lambda i: (i, 0)),
                  pl.BlockSpec((1, d), lambda i: (0, 0))],
        out_specs=pl.BlockSpec((tm, d), lambda i: (i, 0)),
        out_shape=jax.ShapeDtypeStruct((n_rows, d), f32),
        compiler_params=_cparams(1),
        name="gated_out_proj_norm",
    )(o, proj, w, h, g)


def _suffix_sum_sublanes(a):
    sub = lax.broadcasted_iota(jnp.int32, a.shape, 0)
    t = a
    for sh in (1, 2, 4):
        rolled = pltpu.roll(t, SUBLANES - sh, axis=0)
        t = t + jnp.where(sub + sh < SUBLANES, rolled, 0.0)
    return t


def _sb_attention_kernel(q_ref, k_ref, v_ref, o_ref, vt_ref, z_ref, p_ref, at_ref, acc_ref):
    tq = ATT_BLOCK
    dk2 = q_ref.shape[1]
    dv = v_ref.shape[1] // 2
    n_blocks = k_ref.shape[0] // ATT_BLOCK
    heads = range(2)

    for kb in range(n_blocks):
        vb = v_ref[kb * ATT_BLOCK:(kb + 1) * ATT_BLOCK, :].astype(f32)
        vt_ref[kb] = vb.T.astype(bf16)

    lane = lax.broadcasted_iota(jnp.int32, (tq, dk2), 1)
    c = lax.broadcasted_iota(jnp.int32, (SUBLANES, tq), 1)
    i = lax.broadcasted_iota(jnp.int32, (SUBLANES, tq), 0)
    diag_margin = (c % SUBLANES) * ATT_ROWS + c // SUBLANES - i * ATT_ROWS

    def scores(qm, kb, h):
        k_blk = k_ref[pl.ds(pl.multiple_of(kb * ATT_BLOCK, ATT_BLOCK), ATT_BLOCK), :]
        z_ref[h] = lax.dot_general(k_blk, qm[h], (((1,), (1,)), ((), ())), preferred_element_type=f32)

    def pass1(h, masked):
        acc = jnp.zeros((SUBLANES, tq), f32)
        for r in reversed(range(ATT_ROWS)):
            rows = slice(r * SUBLANES, (r + 1) * SUBLANES)
            x = z_ref[h, rows, :]
            if masked:
                x = jnp.where(diag_margin > r, x, MASKED_SCORE)
            e = jnp.exp2(-jnp.abs(x))
            lb = jnp.minimum(x, 0.0) - jnp.log(1.0 + e) * LN2_INV
            p = lb + acc
            acc = acc + (lb - x) if masked else p - x
            p_ref[h, rows, :] = p
        return acc

    def pass2(h, acc, carry):
        suffix = _suffix_sum_sublanes(acc)
        offs = carry + (suffix - acc)
        pieces = [jnp.exp2(p_ref[h, r * SUBLANES:(r + 1) * SUBLANES, :] + offs) for r in range(ATT_ROWS)]
        at_ref[h] = jnp.concatenate(pieces, axis=0).astype(bf16)
        return carry + jnp.broadcast_to(suffix[0:1, :], suffix.shape)

    def weighted_sum(slot, kb, h):
        acc_ref[slot, h] += jnp.dot(vt_ref[kb, h * dv:(h + 1) * dv, :], at_ref[h],
                                    preferred_element_type=f32)

    def step(qm, slot, kb, carry, masked, pending):
        if pending:
            for h in heads:
                weighted_sum(slot, kb + 1, h)
        kb_next = jnp.maximum(kb - 1, 0)
        acc = [pass1(h, masked) for h in heads]
        for h in heads:
            scores(qm, kb_next, h)
        return tuple(pass2(h, acc[h], carry[h]) for h in heads)

    zero_carry = jnp.zeros((SUBLANES, tq), f32)
    for qi in range(n_blocks):
        slot = qi % 2
        rows = slice(qi * tq, (qi + 1) * tq)
        q2 = q_ref[rows, :]
        zero = jnp.zeros_like(q2)
        qm = (jnp.where(lane < dk2 // 2, q2, zero), jnp.where(lane >= dk2 // 2, q2, zero))
        acc_ref[slot] = jnp.zeros(acc_ref.shape[1:], f32)
        for h in heads:
            scores(qm, qi, h)
        carry = step(qm, slot, qi, (zero_carry, zero_carry), True, False)
        n_done = 0
        if qi > 0:
            def live(cr):
                return jnp.max(jnp.maximum(cr[0], cr[1])) > -UNDERFLOW_LOG2

            def cond(state):
                return jnp.logical_and(state[0] < qi, state[1])

            def body(state):
                j, _, ca, cb = state
                cr = step(qm, slot, qi - 1 - j, (ca, cb), False, True)
                return j + 1, live(cr), cr[0], cr[1]

            n_done = lax.while_loop(cond, body, (jnp.int32(0), live(carry), carry[0], carry[1]))[0]
        for h in heads:
            weighted_sum(slot, qi - n_done, h)
            o_ref[rows, h * dv:(h + 1) * dv] = acc_ref[slot, h].T.astype(o_ref.dtype)


def _sb_attention(proj, *, batch, seq, qk_width, d_inner):
    n_rows = proj.shape[0]
    n_pairs = SB_HEADS // 2
    dk2 = 2 * SB_QK_DIM
    dv2 = 2 * (d_inner // SB_HEADS)
    k_col0 = qk_width // dk2
    v_col0 = 2 * qk_width // dv2
    assert seq % ATT_BLOCK == 0 and qk_width == SB_HEADS * SB_QK_DIM and (2 * qk_width) % dv2 == 0
    return pl.pallas_call(
        _sb_attention_kernel,
        grid=(batch, n_pairs),
        in_specs=[pl.BlockSpec((seq, dk2), lambda b, p: (b, p)),
                  pl.BlockSpec((seq, dk2), lambda b, p: (b, k_col0 + p)),
                  pl.BlockSpec((seq, dv2), lambda b, p: (b, v_col0 + p))],
        out_specs=pl.BlockSpec((seq, dv2), lambda b, p: (b, p)),
        out_shape=jax.ShapeDtypeStruct((n_rows, d_inner), bf16),
        scratch_shapes=[pltpu.VMEM((seq // ATT_BLOCK, dv2, ATT_BLOCK), bf16),
                        pltpu.VMEM((2, ATT_BLOCK, ATT_BLOCK), f32),
                        pltpu.VMEM((2, ATT_BLOCK, ATT_BLOCK), f32),
                        pltpu.VMEM((2, ATT_BLOCK, ATT_BLOCK), bf16),
                        pltpu.VMEM((2, 2, dv2 // 2, ATT_BLOCK), f32)],
        compiler_params=_cparams(2),
        name="sb_attention",
    )(proj, proj, proj)


def _permute_tokens(a):
    n, f = a.shape
    return a.reshape(n // ATT_BLOCK, SUBLANES, ATT_ROWS, f).swapaxes(1, 2).reshape(n, f)


def _unpermute_tokens(a):
    n, f = a.shape
    return a.reshape(n // ATT_BLOCK, ATT_ROWS, SUBLANES, f).swapaxes(1, 2).reshape(n, f)


def kernel(x, norm_g, pool_w_in, pool_w, pool_scale, pool_w_out, sb_w_in, sb_w_out, norm_f):
    batch, seq, d = x.shape
    assert norm_g.shape[0] == 2 and pool_w_in.shape[0] == 1 and sb_w_in.shape[0] == 1
    d_inner = pool_w_out.shape[1]
    qk_width = (sb_w_in.shape[2] - 2 * d_inner) // 2
    n_rows = batch * seq
    tm = 512 if seq % 512 == 0 else ATT_BLOCK

    h = x.reshape(n_rows, d)
    proj = _norm_proj(h, norm_g[0].reshape(1, d), pool_w_in[0].astype(bf16), tm=tm, chunk=512)
    y = _pool_mix(proj, pool_w[0].astype(bf16), pool_scale[0].reshape(1, d_inner), tm=tm, seq=seq)
    h = _out_proj(y, pool_w_out[0].astype(bf16), h, tm=tm)

    hp = _permute_tokens(h)
    proj = _norm_proj(hp, norm_g[1].reshape(1, d), sb_w_in[0].astype(bf16), tm=tm, chunk=512,
                      scaled_cols=qk_width, col_scale=LOG2E / math.sqrt(SB_QK_DIM))
    o = _sb_attention(proj, batch=batch, seq=seq, qk_width=qk_width, d_inner=d_inner)
    gate_block = 2 * qk_width // d_inner + 1
    out = _gated_out_proj_norm(o, proj, gate_block, sb_w_out[0].astype(bf16), hp,
                               norm_f.reshape(1, d), tm=tm)
    return _unpermute_tokens(out).reshape(batch, seq, d)
```

```python
import functools
import math

import jax
import jax.numpy as jnp
from jax import lax
from jax.experimental import pallas as pl
from jax.experimental.pallas import tpu as pltpu

RMS_EPS = 1e-6
POOL_WINDOWS = (2, 4, 8, 16)
POOL_HALO = 16
SB_HEADS = 16
SB_QK_DIM = 64
LOG2E = 1.4426950408889634
LN2_INV = LOG2E
MASKED_SCORE = -1e30
UNDERFLOW_LOG2 = 160.0

SUBLANES = 8
ATT_BLOCK = 256
ATT_ROWS = ATT_BLOCK // SUBLANES
VMEM_LIMIT = 56 * 1024 * 1024

f32 = jnp.float32
bf16 = jnp.bfloat16


def _cparams(n_axes):
    return pltpu.CompilerParams(dimension_semantics=("arbitrary",) * n_axes,
                                vmem_limit_bytes=VMEM_LIMIT)


def _norm_proj_kernel(x_ref, g_ref, w_ref, o_ref, *, chunk, scaled_cols, col_scale):
    x = x_ref[...]
    inv = lax.rsqrt(jnp.mean(x * x, axis=-1, keepdims=True) + RMS_EPS)
    u = (x * inv * g_ref[...]).astype(bf16)
    n = o_ref.shape[1]
    for c in range(n // chunk):
        cols = slice(c * chunk, (c + 1) * chunk)
        r = jnp.dot(u, w_ref[:, cols], preferred_element_type=f32)
        if (c + 1) * chunk <= scaled_cols:
            r = r * col_scale
        o_ref[:, cols] = r.astype(o_ref.dtype)


def _norm_proj(h, g, w, *, tm, chunk, scaled_cols=0, col_scale=1.0):
    n_rows, d = h.shape
    n = w.shape[1]
    assert n_rows % tm == 0 and n % chunk == 0 and scaled_cols % chunk == 0
    kern = functools.partial(_norm_proj_kernel, chunk=chunk, scaled_cols=scaled_cols, col_scale=col_scale)
    return pl.pallas_call(
        kern,
        grid=(n_rows // tm,),
        in_specs=[pl.BlockSpec((tm, d), lambda i: (i, 0)),
                  pl.BlockSpec((1, d), lambda i: (0, 0)),
                  pl.BlockSpec((d, n), lambda i: (0, 0))],
        out_specs=pl.BlockSpec((tm, n), lambda i: (i, 0)),
        out_shape=jax.ShapeDtypeStruct((n_rows, n), bf16),
        compiler_params=_cparams(1),
        name="norm_proj",
    )(h, g, w)


LANES = 128


def _stage_columns(stage_ref, value):
    for c in range(stage_ref.shape[0]):
        stage_ref[c] = value[:, c * LANES:(c + 1) * LANES]


def _store_permuted(o_ref, stage_ref):
    for c in range(stage_ref.shape[0]):
        cols = slice(c * LANES, (c + 1) * LANES)
        for b in range(stage_ref.shape[1] // ATT_BLOCK):
            for r in range(ATT_ROWS):
                dst = b * ATT_BLOCK + r * SUBLANES
                o_ref[dst:dst + SUBLANES, cols] = stage_ref[c, pl.ds(b * ATT_BLOCK + r, SUBLANES,
                                                                     stride=ATT_ROWS), :]


def _store_unpermuted(o_ref, stage_ref):
    for c in range(stage_ref.shape[0]):
        cols = slice(c * LANES, (c + 1) * LANES)
        for b in range(stage_ref.shape[1] // ATT_BLOCK):
            for i in range(SUBLANES):
                dst = b * ATT_BLOCK + i * ATT_ROWS
                o_ref[dst:dst + ATT_ROWS, cols] = stage_ref[c, pl.ds(b * ATT_BLOCK + i, ATT_ROWS,
                                                                     stride=SUBLANES), :]


def _pool_layer_kernel(x_ref, g_ref, win_ref, wg_ref, sc_ref, wout_ref, o_ref, xs_ref, y_ref, h_ref,
                       *, tm, seq, windows, gd):
    di = gd * len(windows)
    tile_in_seq = pl.program_id(0) % (seq // tm)

    @pl.when(tile_in_seq == 0)
    def _():
        xs_ref[:POOL_HALO, :] = jnp.zeros((POOL_HALO, di), f32)

    @pl.when(tile_in_seq != 0)
    def _():
        xs_ref[:POOL_HALO, :] = xs_ref[tm:tm + POOL_HALO, :]

    x = x_ref[...]
    inv = lax.rsqrt(jnp.mean(x * x, axis=-1, keepdims=True) + RMS_EPS)
    u = (x * inv * g_ref[...]).astype(bf16)
    for g in range(len(windows)):
        cols = slice(g * gd, (g + 1) * gd)
        xs_ref[POOL_HALO:, cols] = jnp.dot(u, win_ref[:, cols], preferred_element_type=f32)
    pos = tile_in_seq * tm + lax.broadcasted_iota(jnp.int32, (tm, 1), 0)
    for g, w in enumerate(windows):
        cols = slice(g * gd, (g + 1) * gd)
        x_self = xs_ref[POOL_HALO:POOL_HALO + tm, cols]
        s = x_self
        for d in range(1, w):
            s = s + xs_ref[POOL_HALO - d:POOL_HALO - d + tm, cols]
        count = jnp.minimum(pos + 1, w).astype(f32)
        pooled = s / count - x_self
        mixed = jnp.dot(pooled.astype(bf16), wg_ref[g], preferred_element_type=f32)
        z = jnp.dot(u, win_ref[:, di + g * gd:di + (g + 1) * gd], preferred_element_type=f32)
        y = mixed * sc_ref[:, cols] * (z * jax.nn.sigmoid(z))
        y_ref[:, cols] = y.astype(bf16)
    _stage_columns(h_ref, x_ref[...] + jnp.dot(y_ref[...], wout_ref[...], preferred_element_type=f32))
    _store_permuted(o_ref, h_ref)


def _pool_layer(h, g, w_in, wg, scale, w_out, *, tm, seq):
    n_rows, d = h.shape
    n_groups, gd, _ = wg.shape
    di = n_groups * gd
    assert n_groups == len(POOL_WINDOWS) and w_in.shape == (d, 2 * di) and w_out.shape == (di, d)
    assert seq % tm == 0 and tm % ATT_BLOCK == 0 and max(POOL_WINDOWS) <= POOL_HALO
    kern = functools.partial(_pool_layer_kernel, tm=tm, seq=seq, windows=POOL_WINDOWS, gd=gd)
    resident = dict(pipeline_mode=pl.Buffered(1))
    return pl.pallas_call(
        kern,
        grid=(n_rows // tm,),
        in_specs=[pl.BlockSpec((tm, d), lambda i: (i, 0)),
                  pl.BlockSpec((1, d), lambda i: (0, 0)),
                  pl.BlockSpec((d, 2 * di), lambda i: (0, 0), **resident),
                  pl.BlockSpec((n_groups, gd, gd), lambda i: (0, 0, 0), **resident),
                  pl.BlockSpec((1, di), lambda i: (0, 0)),
                  pl.BlockSpec((di, d), lambda i: (0, 0), **resident)],
        out_specs=pl.BlockSpec((tm, d), lambda i: (i, 0)),
        out_shape=jax.ShapeDtypeStruct((n_rows, d), f32),
        scratch_shapes=[pltpu.VMEM((tm + POOL_HALO, di), f32),
                        pltpu.VMEM((tm, di), bf16),
                        pltpu.VMEM((d // LANES, tm, LANES), f32)],
        compiler_params=_cparams(1),
        name="pool_layer",
    )(h, g, w_in, wg, scale, w_out)


def _gated_out_proj_norm_kernel(o_ref, z_ref, w_ref, h_ref, g_ref, out_ref, r_ref):
    z = z_ref[...].astype(f32)
    y = (o_ref[...].astype(f32) * (z * jax.nn.sigmoid(z))).astype(bf16)
    h = h_ref[...] + jnp.dot(y, w_ref[...], preferred_element_type=f32)
    inv = lax.rsqrt(jnp.mean(h * h, axis=-1, keepdims=True) + RMS_EPS)
    _stage_columns(r_ref, h * inv * g_ref[...])
    _store_unpermuted(out_ref, r_ref)


def _gated_out_proj_norm(o, proj, gate_block, w, h, g, *, tm):
    n_rows, di = o.shape
    d = w.shape[1]
    assert tm % ATT_BLOCK == 0
    return pl.pallas_call(
        _gated_out_proj_norm_kernel,
        grid=(n_rows // tm,),
        in_specs=[pl.BlockSpec((tm, di), lambda i: (i, 0)),
                  pl.BlockSpec((tm, di), lambda i: (i, gate_block)),
                  pl.BlockSpec((di, d), lambda i: (0, 0)),
                  pl.BlockSpec((tm, d), lambda i: (i, 0)),
                  pl.BlockSpec((1, d), lambda i: (0, 0))],
        out_specs=pl.BlockSpec((tm, d), lambda i: (i, 0)),
        out_shape=jax.ShapeDtypeStruct((n_rows, d), f32),
        scratch_shapes=[pltpu.VMEM((d // LANES, tm, LANES), f32)],
        compiler_params=_cparams(1),
        name="gated_out_proj_norm",
    )(o, proj, w, h, g)


def _suffix_sum_sublanes(a):
    sub = lax.broadcasted_iota(jnp.int32, a.shape, 0)
    t = a
    for sh in (1, 2, 4):
        rolled = pltpu.roll(t, SUBLANES - sh, axis=0)
        t = t + jnp.where(sub + sh < SUBLANES, rolled, 0.0)
    return t


def _sb_attention_kernel(q_ref, k_ref, v_ref, o_ref, vt_ref, z_ref, p_ref, at_ref, acc_ref):
    tq = ATT_BLOCK
    dk2 = q_ref.shape[1]
    dv = v_ref.shape[1] // 2
    n_blocks = k_ref.shape[0] // ATT_BLOCK
    heads = range(2)

    for kb in range(n_blocks):
        vb = v_ref[kb * ATT_BLOCK:(kb + 1) * ATT_BLOCK, :].astype(f32)
        vt_ref[kb] = vb.T.astype(bf16)

    lane = lax.broadcasted_iota(jnp.int32, (tq, dk2), 1)
    c = lax.broadcasted_iota(jnp.int32, (SUBLANES, tq), 1)
    i = lax.broadcasted_iota(jnp.int32, (SUBLANES, tq), 0)
    diag_margin = (c % SUBLANES) * ATT_ROWS + c // SUBLANES - i * ATT_ROWS

    def scores(qm, kb, h):
        k_blk = k_ref[pl.ds(pl.multiple_of(kb * ATT_BLOCK, ATT_BLOCK), ATT_BLOCK), :]
        z_ref[h] = lax.dot_general(k_blk, qm[h], (((1,), (1,)), ((), ())), preferred_element_type=f32)

    def pass1(h, masked):
        acc = jnp.zeros((SUBLANES, tq), f32)
        for r in reversed(range(ATT_ROWS)):
            rows = slice(r * SUBLANES, (r + 1) * SUBLANES)
            x = z_ref[h, rows, :]
            if masked:
                x = jnp.where(diag_margin > r, x, MASKED_SCORE)
            e = jnp.exp2(-jnp.abs(x))
            lb = jnp.minimum(x, 0.0) - jnp.log(1.0 + e) * LN2_INV
            p = lb + acc
            acc = acc + (lb - x) if masked else p - x
            p_ref[h, rows, :] = p
        return acc

    def pass2(h, acc, carry):
        suffix = _suffix_sum_sublanes(acc)
        offs = carry + (suffix - acc)
        pieces = [jnp.exp2(p_ref[h, r * SUBLANES:(r + 1) * SUBLANES, :] + offs) for r in range(ATT_ROWS)]
        at_ref[h] = jnp.concatenate(pieces, axis=0).astype(bf16)
        return carry + jnp.broadcast_to(suffix[0:1, :], suffix.shape)

    def weighted_sum(slot, kb, h):
        acc_ref[slot, h] += jnp.dot(vt_ref[kb, h * dv:(h + 1) * dv, :], at_ref[h],
                                    preferred_element_type=f32)

    def step(qm, slot, kb, carry, masked, pending):
        if pending:
            for h in heads:
                weighted_sum(slot, kb + 1, h)
        kb_next = jnp.maximum(kb - 1, 0)
        acc = [pass1(h, masked) for h in heads]
        for h in heads:
            scores(qm, kb_next, h)
        return tuple(pass2(h, acc[h], carry[h]) for h in heads)

    zero_carry = jnp.zeros((SUBLANES, tq), f32)
    for qi in range(n_blocks):
        slot = qi % 2
        rows = slice(qi * tq, (qi + 1) * tq)
        q2 = q_ref[rows, :]
        zero = jnp.zeros_like(q2)
        qm = (jnp.where(lane < dk2 // 2, q2, zero), jnp.where(lane >= dk2 // 2, q2, zero))
        acc_ref[slot] = jnp.zeros(acc_ref.shape[1:], f32)
        for h in heads:
            scores(qm, qi, h)
        carry = step(qm, slot, qi, (zero_carry, zero_carry), True, False)
        n_done = 0
        if qi > 0:
            def live(cr):
                return jnp.max(jnp.maximum(cr[0], cr[1])) > -UNDERFLOW_LOG2

            def cond(state):
                return jnp.logical_and(state[0] < qi, state[1])

            def body(state):
                j, _, ca, cb = state
                cr = step(qm, slot, qi - 1 - j, (ca, cb), False, True)
                return j + 1, live(cr), cr[0], cr[1]

            n_done = lax.while_loop(cond, body, (jnp.int32(0), live(carry), carry[0], carry[1]))[0]
        for h in heads:
            weighted_sum(slot, qi - n_done, h)
            o_ref[rows, h * dv:(h + 1) * dv] = acc_ref[slot, h].T.astype(o_ref.dtype)


def _sb_attention(proj, *, batch, seq, qk_width, d_inner):
    n_rows = proj.shape[0]
    n_pairs = SB_HEADS // 2
    dk2 = 2 * SB_QK_DIM
    dv2 = 2 * (d_inner // SB_HEADS)
    k_col0 = qk_width // dk2
    v_col0 = 2 * qk_width // dv2
    assert seq % ATT_BLOCK == 0 and qk_width == SB_HEADS * SB_QK_DIM and (2 * qk_width) % dv2 == 0
    return pl.pallas_call(
        _sb_attention_kernel,
        grid=(batch, n_pairs),
        in_specs=[pl.BlockSpec((seq, dk2), lambda b, p: (b, p)),
                  pl.BlockSpec((seq, dk2), lambda b, p: (b, k_col0 + p)),
                  pl.BlockSpec((seq, dv2), lambda b, p: (b, v_col0 + p))],
        out_specs=pl.BlockSpec((seq, dv2), lambda b, p: (b, p)),
        out_shape=jax.ShapeDtypeStruct((n_rows, d_inner), bf16),
        scratch_shapes=[pltpu.VMEM((seq // ATT_BLOCK, dv2, ATT_BLOCK), bf16),
                        pltpu.VMEM((2, ATT_BLOCK, ATT_BLOCK), f32),
                        pltpu.VMEM((2, ATT_BLOCK, ATT_BLOCK), f32),
                        pltpu.VMEM((2, ATT_BLOCK, ATT_BLOCK), bf16),
                        pltpu.VMEM((2, 2, dv2 // 2, ATT_BLOCK), f32)],
        compiler_params=_cparams(2),
        name="sb_attention",
    )(proj, proj, proj)


def kernel(x, norm_g, pool_w_in, pool_w, pool_scale, pool_w_out, sb_w_in, sb_w_out, norm_f):
    batch, seq, d = x.shape
    assert norm_g.shape[0] == 2 and pool_w_in.shape[0] == 1 and sb_w_in.shape[0] == 1
    d_inner = pool_w_out.shape[1]
    qk_width = (sb_w_in.shape[2] - 2 * d_inner) // 2
    n_rows = batch * seq
    tm = 512 if seq % 512 == 0 else ATT_BLOCK

    h = x.reshape(n_rows, d)
    hp = _pool_layer(h, norm_g[0].reshape(1, d), pool_w_in[0].astype(bf16), pool_w[0].astype(bf16),
                     pool_scale[0].reshape(1, d_inner), pool_w_out[0].astype(bf16), tm=tm, seq=seq)

    proj = _norm_proj(hp, norm_g[1].reshape(1, d), sb_w_in[0].astype(bf16), tm=tm, chunk=512,
                      scaled_cols=qk_width, col_scale=LOG2E / math.sqrt(SB_QK_DIM))
    o = _sb_attention(proj, batch=batch, seq=seq, qk_width=qk_width, d_inner=d_inner)
    gate_block = 2 * qk_width // d_inner + 1
    out = _gated_out_proj_norm(o, proj, gate_block, sb_w_out[0].astype(bf16), hp,
                               norm_f.reshape(1, d), tm=tm)
    return out.reshape(batch, seq, d)
```

```python
import functools
import math

import jax
import jax.numpy as jnp
from jax import lax
from jax.experimental import pallas as pl
from jax.experimental.pallas import tpu as pltpu

RMS_EPS = 1e-6
POOL_WINDOWS = (2, 4, 8, 16)
POOL_HALO = 16
SB_HEADS = 16
SB_QK_DIM = 64
LOG2E = 1.4426950408889634
LN2_INV = LOG2E
MASKED_SCORE = -1e30
UNDERFLOW_LOG2 = 160.0

SUBLANES = 8
ATT_BLOCK = 256
ATT_ROWS = ATT_BLOCK // SUBLANES
VMEM_LIMIT = 56 * 1024 * 1024

f32 = jnp.float32
bf16 = jnp.bfloat16


def _cparams(n_axes):
    return pltpu.CompilerParams(dimension_semantics=("arbitrary",) * n_axes,
                                vmem_limit_bytes=VMEM_LIMIT)


def _norm_proj_kernel(x_ref, g_ref, w_ref, o_ref, *, chunk, scaled_cols, col_scale):
    x = x_ref[...]
    inv = lax.rsqrt(jnp.mean(x * x, axis=-1, keepdims=True) + RMS_EPS)
    u = (x * inv * g_ref[...]).astype(bf16)
    n = o_ref.shape[1]
    for c in range(n // chunk):
        cols = slice(c * chunk, (c + 1) * chunk)
        r = jnp.dot(u, w_ref[:, cols], preferred_element_type=f32)
        if (c + 1) * chunk <= scaled_cols:
            r = r * col_scale
        o_ref[:, cols] = r.astype(o_ref.dtype)


def _norm_proj(h, g, w, *, tm, chunk, scaled_cols=0, col_scale=1.0):
    n_rows, d = h.shape
    n = w.shape[1]
    assert n_rows % tm == 0 and n % chunk == 0 and scaled_cols % chunk == 0
    kern = functools.partial(_norm_proj_kernel, chunk=chunk, scaled_cols=scaled_cols, col_scale=col_scale)
    return pl.pallas_call(
        kern,
        grid=(n_rows // tm,),
        in_specs=[pl.BlockSpec((tm, d), lambda i: (i, 0)),
                  pl.BlockSpec((1, d), lambda i: (0, 0)),
                  pl.BlockSpec((d, n), lambda i: (0, 0))],
        out_specs=pl.BlockSpec((tm, n), lambda i: (i, 0)),
        out_shape=jax.ShapeDtypeStruct((n_rows, n), bf16),
        compiler_params=_cparams(1),
        name="norm_proj",
    )(h, g, w)


LANES = 128


def _stage_columns(stage_ref, value):
    for c in range(stage_ref.shape[0]):
        stage_ref[c] = value[:, c * LANES:(c + 1) * LANES]


def _store_permuted(o_ref, stage_ref):
    for c in range(stage_ref.shape[0]):
        cols = slice(c * LANES, (c + 1) * LANES)
        for b in range(stage_ref.shape[1] // ATT_BLOCK):
            for r in range(ATT_ROWS):
                dst = b * ATT_BLOCK + r * SUBLANES
                o_ref[dst:dst + SUBLANES, cols] = stage_ref[c, pl.ds(b * ATT_BLOCK + r, SUBLANES,
                                                                     stride=ATT_ROWS), :]


def _store_unpermuted(o_ref, stage_ref):
    for c in range(stage_ref.shape[0]):
        cols = slice(c * LANES, (c + 1) * LANES)
        for b in range(stage_ref.shape[1] // ATT_BLOCK):
            for i in range(SUBLANES):
                dst = b * ATT_BLOCK + i * ATT_ROWS
                o_ref[dst:dst + ATT_ROWS, cols] = stage_ref[c, pl.ds(b * ATT_BLOCK + i, ATT_ROWS,
                                                                     stride=SUBLANES), :]


def _pool_layer_kernel(x_ref, g_ref, win_ref, wg_ref, sc_ref, wout_ref, o_ref, xs_ref, y_ref, h_ref,
                       *, tm, seq, windows, gd):
    di = gd * len(windows)
    tile_in_seq = pl.program_id(0) % (seq // tm)

    @pl.when(tile_in_seq == 0)
    def _():
        xs_ref[:POOL_HALO, :] = jnp.zeros((POOL_HALO, di), f32)

    @pl.when(tile_in_seq != 0)
    def _():
        xs_ref[:POOL_HALO, :] = xs_ref[tm:tm + POOL_HALO, :]

    x = x_ref[...]
    inv = lax.rsqrt(jnp.mean(x * x, axis=-1, keepdims=True) + RMS_EPS)
    u = (x * inv * g_ref[...]).astype(bf16)
    for g in range(len(windows)):
        cols = slice(g * gd, (g + 1) * gd)
        xs_ref[POOL_HALO:, cols] = jnp.dot(u, win_ref[:, cols], preferred_element_type=f32)
    pos = tile_in_seq * tm + lax.broadcasted_iota(jnp.int32, (tm, 1), 0)
    for g, w in enumerate(windows):
        cols = slice(g * gd, (g + 1) * gd)
        x_self = xs_ref[POOL_HALO:POOL_HALO + tm, cols]
        s = x_self
        for d in range(1, w):
            s = s + xs_ref[POOL_HALO - d:POOL_HALO - d + tm, cols]
        count = jnp.minimum(pos + 1, w).astype(f32)
        pooled = s / count - x_self
        mixed = jnp.dot(pooled.astype(bf16), wg_ref[g], preferred_element_type=f32)
        z = jnp.dot(u, win_ref[:, di + g * gd:di + (g + 1) * gd], preferred_element_type=f32)
        y = mixed * sc_ref[:, cols] * (z * jax.nn.sigmoid(z))
        y_ref[:, cols] = y.astype(bf16)
    _stage_columns(h_ref, x_ref[...] + jnp.dot(y_ref[...], wout_ref[...], preferred_element_type=f32))
    _store_permuted(o_ref, h_ref)


def _pool_layer(h, g, w_in, wg, scale, w_out, *, tm, seq):
    n_rows, d = h.shape
    n_groups, gd, _ = wg.shape
    di = n_groups * gd
    assert n_groups == len(POOL_WINDOWS) and w_in.shape == (d, 2 * di) and w_out.shape == (di, d)
    assert seq % tm == 0 and tm % ATT_BLOCK == 0 and max(POOL_WINDOWS) <= POOL_HALO
    kern = functools.partial(_pool_layer_kernel, tm=tm, seq=seq, windows=POOL_WINDOWS, gd=gd)
    resident = dict(pipeline_mode=pl.Buffered(1))
    return pl.pallas_call(
        kern,
        grid=(n_rows // tm,),
        in_specs=[pl.BlockSpec((tm, d), lambda i: (i, 0)),
                  pl.BlockSpec((1, d), lambda i: (0, 0)),
                  pl.BlockSpec((d, 2 * di), lambda i: (0, 0), **resident),
                  pl.BlockSpec((n_groups, gd, gd), lambda i: (0, 0, 0), **resident),
                  pl.BlockSpec((1, di), lambda i: (0, 0)),
                  pl.BlockSpec((di, d), lambda i: (0, 0), **resident)],
        out_specs=pl.BlockSpec((tm, d), lambda i: (i, 0)),
        out_shape=jax.ShapeDtypeStruct((n_rows, d), f32),
        scratch_shapes=[pltpu.VMEM((tm + POOL_HALO, di), f32),
                        pltpu.VMEM((tm, di), bf16),
                        pltpu.VMEM((d // LANES, tm, LANES), f32)],
        compiler_params=_cparams(1),
        name="pool_layer",
    )(h, g, w_in, wg, scale, w_out)


def _gated_out_proj_norm_kernel(o_ref, z_ref, w_ref, h_ref, g_ref, out_ref, r_ref):
    z = z_ref[...].astype(f32)
    y = (o_ref[...].astype(f32) * (z * jax.nn.sigmoid(z))).astype(bf16)
    h = h_ref[...] + jnp.dot(y, w_ref[...], preferred_element_type=f32)
    inv = lax.rsqrt(jnp.mean(h * h, axis=-1, keepdims=True) + RMS_EPS)
    _stage_columns(r_ref, h * inv * g_ref[...])
    _store_unpermuted(out_ref, r_ref)


def _gated_out_proj_norm(o, proj, gate_block, w, h, g, *, tm):
    n_rows, di = o.shape
    d = w.shape[1]
    assert tm % ATT_BLOCK == 0
    return pl.pallas_call(
        _gated_out_proj_norm_kernel,
        grid=(n_rows // tm,),
        in_specs=[pl.BlockSpec((tm, di), lambda i: (i, 0)),
                  pl.BlockSpec((tm, di), lambda i: (i, gate_block)),
                  pl.BlockSpec((di, d), lambda i: (0, 0)),
                  pl.BlockSpec((tm, d), lambda i: (i, 0)),
                  pl.BlockSpec((1, d), lambda i: (0, 0))],
        out_specs=pl.BlockSpec((tm, d), lambda i: (i, 0)),
        out_shape=jax.ShapeDtypeStruct((n_rows, d), f32),
        scratch_shapes=[pltpu.VMEM((d // LANES, tm, LANES), f32)],
        compiler_params=_cparams(1),
        name="gated_out_proj_norm",
    )(o, proj, w, h, g)


def _suffix_sum_sublanes(a):
    sub = lax.broadcasted_iota(jnp.int32, a.shape, 0)
    t = a
    for sh in (1, 2, 4):
        rolled = pltpu.roll(t, SUBLANES - sh, axis=0)
        t = t + jnp.where(sub + sh < SUBLANES, rolled, 0.0)
    return t


def _sb_attention_kernel(q_ref, k_ref, v_ref, o_ref, vt_ref, *scratch):
    tq = ATT_BLOCK
    dk2 = q_ref.shape[1]
    dv = v_ref.shape[1] // 2
    n_blocks = k_ref.shape[0] // ATT_BLOCK
    heads = range(2)
    pairs = [scratch[n:n + 2] for n in range(0, len(scratch), 2)]
    zd_refs, zo0_ref, zo1_ref, p_ref, at_ref, acc_ref = (pairs[0:2], pairs[2], pairs[3], pairs[4], pairs[5],
                                                         pairs[6:8])

    for kb in range(n_blocks):
        vb = v_ref[kb * ATT_BLOCK:(kb + 1) * ATT_BLOCK, :].astype(f32)
        vt_ref[kb] = vb.T.astype(bf16)

    lane = lax.broadcasted_iota(jnp.int32, (tq, dk2), 1)
    c = lax.broadcasted_iota(jnp.int32, (SUBLANES, tq), 1)
    i = lax.broadcasted_iota(jnp.int32, (SUBLANES, tq), 0)
    diag_margin = (c % SUBLANES) * ATT_ROWS + c // SUBLANES - i * ATT_ROWS

    def masked_queries(qi):
        q2 = q_ref[qi * tq:(qi + 1) * tq, :]
        zero = jnp.zeros_like(q2)
        return jnp.where(lane < dk2 // 2, q2, zero), jnp.where(lane >= dk2 // 2, q2, zero)

    def scores(z_ref, qm, kb):
        start = kb * ATT_BLOCK if isinstance(kb, int) else pl.multiple_of(kb * ATT_BLOCK, ATT_BLOCK)
        k_blk = k_ref[pl.ds(start, ATT_BLOCK), :]
        for h in heads:
            z_ref[h][...] = lax.dot_general(k_blk, qm[h], (((1,), (1,)), ((), ())),
                                            preferred_element_type=f32)

    def pass1(z_ref, h, masked):
        acc = jnp.zeros((SUBLANES, tq), f32)
        for r in reversed(range(ATT_ROWS)):
            rows = slice(r * SUBLANES, (r + 1) * SUBLANES)
            x = z_ref[h][rows, :]
            if masked:
                x = jnp.where(diag_margin > r, x, MASKED_SCORE)
            e = jnp.exp2(-jnp.abs(x))
            lb = jnp.minimum(x, 0.0) - jnp.log(1.0 + e) * LN2_INV
            p = lb + acc
            acc = acc + (lb - x) if masked else p - x
            p_ref[h][rows, :] = p
        return acc

    def pass2(h, acc, carry):
        suffix = _suffix_sum_sublanes(acc)
        offs = carry + (suffix - acc)
        pieces = [jnp.exp2(p_ref[h][r * SUBLANES:(r + 1) * SUBLANES, :] + offs) for r in range(ATT_ROWS)]
        at_ref[h][...] = jnp.concatenate(pieces, axis=0).astype(bf16)
        return carry + jnp.broadcast_to(suffix[0:1, :], suffix.shape)

    def weighted_sum(slot, kb, h):
        acc_ref[slot][h][...] += jnp.dot(vt_ref[kb, h * dv:(h + 1) * dv, :], at_ref[h][...],
                                         preferred_element_type=f32)

    def block(z_ref, carry, masked):
        acc = [pass1(z_ref, h, masked) for h in heads]
        return tuple(pass2(h, acc[h], carry[h]) for h in heads)

    zero_carry = jnp.zeros((SUBLANES, tq), f32)
    qm_next = masked_queries(0)
    scores(zd_refs[0], qm_next, 0)
    for qi in range(n_blocks):
        slot = qi % 2
        qm = qm_next
        if qi > 0:
            scores(zo0_ref, qm, qi - 1)
        if qi + 1 < n_blocks:
            qm_next = masked_queries(qi + 1)
            scores(zd_refs[1 - slot], qm_next, qi + 1)
        if qi > 1:
            scores(zo1_ref, qm, qi - 2)
        for h in heads:
            acc_ref[slot][h][...] = jnp.zeros(acc_ref[slot][h].shape, f32)
        carry = block(zd_refs[slot], (zero_carry, zero_carry), True)
        n_done = 0
        if qi > 0:
            for h in heads:
                weighted_sum(slot, qi, h)
            carry = block(zo0_ref, carry, False)
            n_done = 1
        if qi > 1:
            z_loop = zo1_ref

            def live(cr):
                return jnp.max(jnp.maximum(cr[0], cr[1])) > -UNDERFLOW_LOG2

            def cond(state):
                return jnp.logical_and(state[0] < qi, state[1])

            def body(state):
                j, _, ca, cb = state
                kb = qi - 1 - j
                for h in heads:
                    weighted_sum(slot, kb + 1, h)
                acc = [pass1(z_loop, h, False) for h in heads]
                scores(z_loop, qm, jnp.maximum(kb - 1, 0))
                cr = tuple(pass2(h, acc[h], c_h) for h, c_h in zip(heads, (ca, cb)))
                return j + 1, live(cr), cr[0], cr[1]

            n_done = lax.while_loop(cond, body, (jnp.int32(1), live(carry), carry[0], carry[1]))[0]
        rows = slice(qi * tq, (qi + 1) * tq)
        for h in heads:
            weighted_sum(slot, qi - n_done, h)
            o_ref[rows, h * dv:(h + 1) * dv] = acc_ref[slot][h][...].T.astype(o_ref.dtype)


def _sb_attention(proj, *, batch, seq, qk_width, d_inner):
    n_rows = proj.shape[0]
    n_pairs = SB_HEADS // 2
    dk2 = 2 * SB_QK_DIM
    dv2 = 2 * (d_inner // SB_HEADS)
    k_col0 = qk_width // dk2
    v_col0 = 2 * qk_width // dv2
    assert seq % ATT_BLOCK == 0 and qk_width == SB_HEADS * SB_QK_DIM and (2 * qk_width) % dv2 == 0
    return pl.pallas_call(
        _sb_attention_kernel,
        grid=(batch, n_pairs),
        in_specs=[pl.BlockSpec((seq, dk2), lambda b, p: (b, p)),
                  pl.BlockSpec((seq, dk2), lambda b, p: (b, k_col0 + p)),
                  pl.BlockSpec((seq, dv2), lambda b, p: (b, v_col0 + p))],
        out_specs=pl.BlockSpec((seq, dv2), lambda b, p: (b, p)),
        out_shape=jax.ShapeDtypeStruct((n_rows, d_inner), bf16),
        scratch_shapes=([pltpu.VMEM((seq // ATT_BLOCK, dv2, ATT_BLOCK), bf16)]
                        + [pltpu.VMEM((ATT_BLOCK, ATT_BLOCK), f32)] * 10
                        + [pltpu.VMEM((ATT_BLOCK, ATT_BLOCK), bf16)] * 2
                        + [pltpu.VMEM((dv2 // 2, ATT_BLOCK), f32)] * 4),
        compiler_params=_cparams(2),
        name="sb_attention",
    )(proj, proj, proj)


def kernel(x, norm_g, pool_w_in, pool_w, pool_scale, pool_w_out, sb_w_in, sb_w_out, norm_f):
    batch, seq, d = x.shape
    assert norm_g.shape[0] == 2 and pool_w_in.shape[0] == 1 and sb_w_in.shape[0] == 1
    d_inner = pool_w_out.shape[1]
    qk_width = (sb_w_in.shape[2] - 2 * d_inner) // 2
    n_rows = batch * seq
    tm = 512 if seq % 512 == 0 else ATT_BLOCK

    h = x.reshape(n_rows, d)
    hp = _pool_layer(h, norm_g[0].reshape(1, d), pool_w_in[0].astype(bf16), pool_w[0].astype(bf16),
                     pool_scale[0].reshape(1, d_inner), pool_w_out[0].astype(bf16), tm=tm, seq=seq)

    proj = _norm_proj(hp, norm_g[1].reshape(1, d), sb_w_in[0].astype(bf16), tm=tm, chunk=512,
                      scaled_cols=qk_width, col_scale=LOG2E / math.sqrt(SB_QK_DIM))
    o = _sb_attention(proj, batch=batch, seq=seq, qk_width=qk_width, d_inner=d_inner)
    gate_block = 2 * qk_width // d_inner + 1
    out = _gated_out_proj_norm(o, proj, gate_block, sb_w_out[0].astype(bf16), hp,
                               norm_f.reshape(1, d), tm=tm)
    return out.reshape(batch, seq, d)
```

```python
import functools
import math

import jax
import jax.numpy as jnp
from jax import lax
from jax.experimental import pallas as pl
from jax.experimental.pallas import tpu as pltpu

RMS_EPS = 1e-6
POOL_WINDOWS = (2, 4, 8, 16)
POOL_HALO = 16
SB_HEADS = 16
SB_QK_DIM = 64
LOG2E = 1.4426950408889634
LN2_INV = LOG2E
MASKED_SCORE = -1e30
UNDERFLOW_LOG2 = 160.0

SUBLANES = 8
ATT_BLOCK = 256
ATT_ROWS = ATT_BLOCK // SUBLANES
VMEM_LIMIT = 56 * 1024 * 1024

f32 = jnp.float32
bf16 = jnp.bfloat16


def _cparams(n_axes):
    return pltpu.CompilerParams(dimension_semantics=("arbitrary",) * n_axes,
                                vmem_limit_bytes=VMEM_LIMIT)


def _norm_proj_kernel(x_ref, g_ref, w_ref, o_ref, *, chunk, scaled_cols, col_scale):
    x = x_ref[...]
    inv = lax.rsqrt(jnp.mean(x * x, axis=-1, keepdims=True) + RMS_EPS)
    u = (x * inv * g_ref[...]).astype(bf16)
    n = o_ref.shape[1]
    for c in range(n // chunk):
        cols = slice(c * chunk, (c + 1) * chunk)
        r = jnp.dot(u, w_ref[:, cols], preferred_element_type=f32)
        if (c + 1) * chunk <= scaled_cols:
            r = r * col_scale
        o_ref[:, cols] = r.astype(o_ref.dtype)


def _norm_proj(h, g, w, *, tm, chunk, scaled_cols=0, col_scale=1.0):
    n_rows, d = h.shape
    n = w.shape[1]
    assert n_rows % tm == 0 and n % chunk == 0 and scaled_cols % chunk == 0
    kern = functools.partial(_norm_proj_kernel, chunk=chunk, scaled_cols=scaled_cols, col_scale=col_scale)
    return pl.pallas_call(
        kern,
        grid=(n_rows // tm,),
        in_specs=[pl.BlockSpec((tm, d), lambda i: (i, 0)),
                  pl.BlockSpec((1, d), lambda i: (0, 0)),
                  pl.BlockSpec((d, n), lambda i: (0, 0))],
        out_specs=pl.BlockSpec((tm, n), lambda i: (i, 0)),
        out_shape=jax.ShapeDtypeStruct((n_rows, n), bf16),
        compiler_params=_cparams(1),
        name="norm_proj",
    )(h, g, w)


def _pool_layer_kernel(x_ref, g_ref, win_ref, wg_ref, sc_ref, wout_ref, o_ref, xs_ref, y_ref,
                       *, tm, seq, windows, gd):
    di = gd * len(windows)
    tile_in_seq = pl.program_id(0) % (seq // tm)

    @pl.when(tile_in_seq == 0)
    def _():
        xs_ref[:POOL_HALO, :] = jnp.zeros((POOL_HALO, di), f32)

    @pl.when(tile_in_seq != 0)
    def _():
        xs_ref[:POOL_HALO, :] = xs_ref[tm:tm + POOL_HALO, :]

    x = x_ref[...]
    inv = lax.rsqrt(jnp.mean(x * x, axis=-1, keepdims=True) + RMS_EPS)
    u = (x * inv * g_ref[...]).astype(bf16)
    for g in range(len(windows)):
        cols = slice(g * gd, (g + 1) * gd)
        xs_ref[POOL_HALO:, cols] = jnp.dot(u, win_ref[:, cols], preferred_element_type=f32)
    pos = tile_in_seq * tm + lax.broadcasted_iota(jnp.int32, (tm, 1), 0)
    for g, w in enumerate(windows):
        cols = slice(g * gd, (g + 1) * gd)
        x_self = xs_ref[POOL_HALO:POOL_HALO + tm, cols]
        s = x_self
        for d in range(1, w):
            s = s + xs_ref[POOL_HALO - d:POOL_HALO - d + tm, cols]
        count = jnp.minimum(pos + 1, w).astype(f32)
        pooled = s / count - x_self
        mixed = jnp.dot(pooled.astype(bf16), wg_ref[g], preferred_element_type=f32)
        z = jnp.dot(u, win_ref[:, di + g * gd:di + (g + 1) * gd], preferred_element_type=f32)
        y = mixed * sc_ref[:, cols] * (z * jax.nn.sigmoid(z))
        y_ref[:, cols] = y.astype(bf16)
    o_ref[...] = x_ref[...] + jnp.dot(y_ref[...], wout_ref[...], preferred_element_type=f32)


def _pool_layer(h, g, w_in, wg, scale, w_out, *, tm, seq):
    n_rows, d = h.shape
    n_groups, gd, _ = wg.shape
    di = n_groups * gd
    assert n_groups == len(POOL_WINDOWS) and w_in.shape == (d, 2 * di) and w_out.shape == (di, d)
    assert seq % tm == 0 and max(POOL_WINDOWS) <= POOL_HALO
    kern = functools.partial(_pool_layer_kernel, tm=tm, seq=seq, windows=POOL_WINDOWS, gd=gd)
    resident = dict(pipeline_mode=pl.Buffered(1))
    return pl.pallas_call(
        kern,
        grid=(n_rows // tm,),
        in_specs=[pl.BlockSpec((tm, d), lambda i: (i, 0)),
                  pl.BlockSpec((1, d), lambda i: (0, 0)),
                  pl.BlockSpec((d, 2 * di), lambda i: (0, 0), **resident),
                  pl.BlockSpec((n_groups, gd, gd), lambda i: (0, 0, 0), **resident),
                  pl.BlockSpec((1, di), lambda i: (0, 0)),
                  pl.BlockSpec((di, d), lambda i: (0, 0), **resident)],
        out_specs=pl.BlockSpec((tm, d), lambda i: (i, 0)),
        out_shape=jax.ShapeDtypeStruct((n_rows, d), f32),
        scratch_shapes=[pltpu.VMEM((tm + POOL_HALO, di), f32),
                        pltpu.VMEM((tm, di), bf16)],
        compiler_params=_cparams(1),
        name="pool_layer",
    )(h, g, w_in, wg, scale, w_out)


def _gated_out_proj_norm_kernel(o_ref, z_ref, w_ref, h_ref, g_ref, out_ref):
    z = z_ref[...].astype(f32)
    y = (o_ref[...].astype(f32) * (z * jax.nn.sigmoid(z))).astype(bf16)
    h = h_ref[...] + jnp.dot(y, w_ref[...], preferred_element_type=f32)
    inv = lax.rsqrt(jnp.mean(h * h, axis=-1, keepdims=True) + RMS_EPS)
    out_ref[...] = h * inv * g_ref[...]


def _gated_out_proj_norm(o, proj, gate_block, w, h, g, *, tm):
    n_rows, di = o.shape
    d = w.shape[1]
    return pl.pallas_call(
        _gated_out_proj_norm_kernel,
        grid=(n_rows // tm,),
        in_specs=[pl.BlockSpec((tm, di), lambda i: (i, 0)),
                  pl.BlockSpec((tm, di), lambda i: (i, gate_block)),
                  pl.BlockSpec((di, d), lambda i: (0, 0)),
                  pl.BlockSpec((tm, d), lambda i: (i, 0)),
                  pl.BlockSpec((1, d), lambda i: (0, 0))],
        out_specs=pl.BlockSpec((tm, d), lambda i: (i, 0)),
        out_shape=jax.ShapeDtypeStruct((n_rows, d), f32),
        compiler_params=_cparams(1),
        name="gated_out_proj_norm",
    )(o, proj, w, h, g)


def _suffix_sum_sublanes(a):
    sub = lax.broadcasted_iota(jnp.int32, a.shape, 0)
    t = a
    for sh in (1, 2, 4):
        rolled = pltpu.roll(t, SUBLANES - sh, axis=0)
        t = t + jnp.where(sub + sh < SUBLANES, rolled, 0.0)
    return t


def _sb_attention_kernel(q_ref, k_ref, v_ref, o_ref, kp_ref, vt_ref, *scratch):
    tq = ATT_BLOCK
    dk2 = q_ref.shape[1]
    dv = v_ref.shape[1] // 2
    n_blocks = k_ref.shape[0] // ATT_BLOCK
    heads = range(2)
    pairs = [scratch[n:n + 2] for n in range(0, len(scratch), 2)]
    zd_refs, zo0_ref, zo1_ref, p_ref, at_ref, acc_ref = (pairs[0:2], pairs[2], pairs[3], pairs[4], pairs[5],
                                                         pairs[6:8])

    p_idx = lax.broadcasted_iota(jnp.int32, (ATT_BLOCK, ATT_BLOCK), 0)
    s_idx = lax.broadcasted_iota(jnp.int32, (ATT_BLOCK, ATT_BLOCK), 1)
    perm = (s_idx == (p_idx % SUBLANES) * ATT_ROWS + p_idx // SUBLANES).astype(bf16)
    for kb in range(n_blocks):
        blk = slice(kb * ATT_BLOCK, (kb + 1) * ATT_BLOCK)
        kp_ref[blk, :] = jnp.dot(perm, k_ref[blk, :], preferred_element_type=f32).astype(bf16)
        vt_ref[kb] = jnp.dot(perm, v_ref[blk, :], preferred_element_type=f32).astype(bf16).T

    lane = lax.broadcasted_iota(jnp.int32, (tq, dk2), 1)
    c = lax.broadcasted_iota(jnp.int32, (SUBLANES, tq), 1)
    i = lax.broadcasted_iota(jnp.int32, (SUBLANES, tq), 0)
    diag_margin = c - i * ATT_ROWS

    def masked_queries(qi):
        q2 = q_ref[qi * tq:(qi + 1) * tq, :]
        zero = jnp.zeros_like(q2)
        return jnp.where(lane < dk2 // 2, q2, zero), jnp.where(lane >= dk2 // 2, q2, zero)

    def scores(z_ref, qm, kb):
        start = kb * ATT_BLOCK if isinstance(kb, int) else pl.multiple_of(kb * ATT_BLOCK, ATT_BLOCK)
        k_blk = kp_ref[pl.ds(start, ATT_BLOCK), :]
        for h in heads:
            z_ref[h][...] = lax.dot_general(k_blk, qm[h], (((1,), (1,)), ((), ())),
                                            preferred_element_type=f32)

    def pass1(z_ref, h, masked):
        acc = jnp.zeros((SUBLANES, tq), f32)
        for r in reversed(range(ATT_ROWS)):
            rows = slice(r * SUBLANES, (r + 1) * SUBLANES)
            x = z_ref[h][rows, :]
            if masked:
                x = jnp.where(diag_margin > r, x, MASKED_SCORE)
            e = jnp.exp2(-jnp.abs(x))
            lb = jnp.minimum(x, 0.0) - jnp.log(1.0 + e) * LN2_INV
            p = lb + acc
            acc = acc + (lb - x) if masked else p - x
            p_ref[h][rows, :] = p
        return acc

    def pass2(h, acc, carry):
        suffix = _suffix_sum_sublanes(acc)
        offs = carry + (suffix - acc)
        pieces = [jnp.exp2(p_ref[h][r * SUBLANES:(r + 1) * SUBLANES, :] + offs) for r in range(ATT_ROWS)]
        at_ref[h][...] = jnp.concatenate(pieces, axis=0).astype(bf16)
        return carry + jnp.broadcast_to(suffix[0:1, :], suffix.shape)

    def weighted_sum(slot, kb, h):
        acc_ref[slot][h][...] += jnp.dot(vt_ref[kb, h * dv:(h + 1) * dv, :], at_ref[h][...],
                                         preferred_element_type=f32)

    def block(z_ref, carry, masked):
        acc = [pass1(z_ref, h, masked) for h in heads]
        return tuple(pass2(h, acc[h], carry[h]) for h in heads)

    zero_carry = jnp.zeros((SUBLANES, tq), f32)
    qm_next = masked_queries(0)
    scores(zd_refs[0], qm_next, 0)
    for qi in range(n_blocks):
        slot = qi % 2
        qm = qm_next
        if qi > 0:
            scores(zo0_ref, qm, qi - 1)
        if qi + 1 < n_blocks:
            qm_next = masked_queries(qi + 1)
            scores(zd_refs[1 - slot], qm_next, qi + 1)
        if qi > 1:
            scores(zo1_ref, qm, qi - 2)
        for h in heads:
            acc_ref[slot][h][...] = jnp.zeros(acc_ref[slot][h].shape, f32)
        carry = block(zd_refs[slot], (zero_carry, zero_carry), True)
        n_done = 0
        if qi > 0:
            for h in heads:
                weighted_sum(slot, qi, h)
            carry = block(zo0_ref, carry, False)
            n_done = 1
        if qi > 1:
            z_loop = zo1_ref

            def live(cr):
                return jnp.max(jnp.maximum(cr[0], cr[1])) > -UNDERFLOW_LOG2

            def cond(state):
                return jnp.logical_and(state[0] < qi, state[1])

            def body(state):
                j, _, ca, cb = state
                kb = qi - 1 - j
                for h in heads:
                    weighted_sum(slot, kb + 1, h)
                acc = [pass1(z_loop, h, False) for h in heads]
                scores(z_loop, qm, jnp.maximum(kb - 1, 0))
                cr = tuple(pass2(h, acc[h], c_h) for h, c_h in zip(heads, (ca, cb)))
                return j + 1, live(cr), cr[0], cr[1]

            n_done = lax.while_loop(cond, body, (jnp.int32(1), live(carry), carry[0], carry[1]))[0]
        rows = slice(qi * tq, (qi + 1) * tq)
        for h in heads:
            weighted_sum(slot, qi - n_done, h)
            o_ref[rows, h * dv:(h + 1) * dv] = acc_ref[slot][h][...].T.astype(o_ref.dtype)


def _sb_attention(proj, *, batch, seq, qk_width, d_inner):
    n_rows = proj.shape[0]
    n_pairs = SB_HEADS // 2
    dk2 = 2 * SB_QK_DIM
    dv2 = 2 * (d_inner // SB_HEADS)
    k_col0 = qk_width // dk2
    v_col0 = 2 * qk_width // dv2
    assert seq % ATT_BLOCK == 0 and qk_width == SB_HEADS * SB_QK_DIM and (2 * qk_width) % dv2 == 0
    return pl.pallas_call(
        _sb_attention_kernel,
        grid=(batch, n_pairs),
        in_specs=[pl.BlockSpec((seq, dk2), lambda b, p: (b, p)),
                  pl.BlockSpec((seq, dk2), lambda b, p: (b, k_col0 + p)),
                  pl.BlockSpec((seq, dv2), lambda b, p: (b, v_col0 + p))],
        out_specs=pl.BlockSpec((seq, dv2), lambda b, p: (b, p)),
        out_shape=jax.ShapeDtypeStruct((n_rows, d_inner), bf16),
        scratch_shapes=([pltpu.VMEM((seq, dk2), bf16), pltpu.VMEM((seq // ATT_BLOCK, dv2, ATT_BLOCK), bf16)]
                        + [pltpu.VMEM((ATT_BLOCK, ATT_BLOCK), f32)] * 10
                        + [pltpu.VMEM((ATT_BLOCK, ATT_BLOCK), bf16)] * 2
                        + [pltpu.VMEM((dv2 // 2, ATT_BLOCK), f32)] * 4),
        compiler_params=_cparams(2),
        name="sb_attention",
    )(proj, proj, proj)


def kernel(x, norm_g, pool_w_in, pool_w, pool_scale, pool_w_out, sb_w_in, sb_w_out, norm_f):
    batch, seq, d = x.shape
    assert norm_g.shape[0] == 2 and pool_w_in.shape[0] == 1 and sb_w_in.shape[0] == 1
    d_inner = pool_w_out.shape[1]
    qk_width = (sb_w_in.shape[2] - 2 * d_inner) // 2
    n_rows = batch * seq
    tm = 512 if seq % 512 == 0 else ATT_BLOCK

    h = x.reshape(n_rows, d)
    h = _pool_layer(h, norm_g[0].reshape(1, d), pool_w_in[0].astype(bf16), pool_w[0].astype(bf16),
                    pool_scale[0].reshape(1, d_inner), pool_w_out[0].astype(bf16), tm=tm, seq=seq)

    proj = _norm_proj(h, norm_g[1].reshape(1, d), sb_w_in[0].astype(bf16), tm=tm, chunk=512,
                      scaled_cols=qk_width, col_scale=LOG2E / math.sqrt(SB_QK_DIM))
    o = _sb_attention(proj, batch=batch, seq=seq, qk_width=qk_width, d_inner=d_inner)
    gate_block = 2 * qk_width // d_inner + 1
    out = _gated_out_proj_norm(o, proj, gate_block, sb_w_out[0].astype(bf16), h,
                               norm_f.reshape(1, d), tm=tm)
    return out.reshape(batch, seq, d)
```

```python
import functools
import math

import jax
import jax.numpy as jnp
from jax import lax
from jax.experimental import pallas as pl
from jax.experimental.pallas import tpu as pltpu

RMS_EPS = 1e-6
POOL_WINDOWS = (2, 4, 8, 16)
POOL_HALO = 16
SB_HEADS = 16
SB_QK_DIM = 64
LOG2E = 1.4426950408889634
LN2_INV = LOG2E
MASKED_SCORE = -1e30
UNDERFLOW_LOG2 = 160.0

SUBLANES = 8
ATT_BLOCK = 256
ATT_ROWS = ATT_BLOCK // SUBLANES
VMEM_LIMIT = 56 * 1024 * 1024

f32 = jnp.float32
bf16 = jnp.bfloat16


def _cparams(n_axes):
    return pltpu.CompilerParams(dimension_semantics=("arbitrary",) * n_axes,
                                vmem_limit_bytes=VMEM_LIMIT)


def _norm_proj_kernel(x_ref, g_ref, w_ref, o_ref, *, chunk, scaled_cols, col_scale):
    x = x_ref[...]
    inv = lax.rsqrt(jnp.mean(x * x, axis=-1, keepdims=True) + RMS_EPS)
    u = (x * inv * g_ref[...]).astype(bf16)
    n = o_ref.shape[1]
    for c in range(n // chunk):
        cols = slice(c * chunk, (c + 1) * chunk)
        r = jnp.dot(u, w_ref[:, cols], preferred_element_type=f32)
        if (c + 1) * chunk <= scaled_cols:
            r = r * col_scale
        o_ref[:, cols] = r.astype(o_ref.dtype)


def _norm_proj(h, g, w, *, tm, chunk, scaled_cols=0, col_scale=1.0):
    n_rows, d = h.shape
    n = w.shape[1]
    assert n_rows % tm == 0 and n % chunk == 0 and scaled_cols % chunk == 0
    kern = functools.partial(_norm_proj_kernel, chunk=chunk, scaled_cols=scaled_cols, col_scale=col_scale)
    return pl.pallas_call(
        kern,
        grid=(n_rows // tm,),
        in_specs=[pl.BlockSpec((tm, d), lambda i: (i, 0)),
                  pl.BlockSpec((1, d), lambda i: (0, 0)),
                  pl.BlockSpec((d, n), lambda i: (0, 0))],
        out_specs=pl.BlockSpec((tm, n), lambda i: (i, 0)),
        out_shape=jax.ShapeDtypeStruct((n_rows, n), bf16),
        compiler_params=_cparams(1),
        name="norm_proj",
    )(h, g, w)


def _pool_layer_kernel(x_ref, g_ref, win_ref, wg_ref, sc_ref, wout_ref, o_ref, xs_ref, y_ref,
                       *, tm, seq, windows, gd):
    di = gd * len(windows)
    tile_in_seq = pl.program_id(0) % (seq // tm)

    @pl.when(tile_in_seq == 0)
    def _():
        xs_ref[:POOL_HALO, :] = jnp.zeros((POOL_HALO, di), f32)

    @pl.when(tile_in_seq != 0)
    def _():
        xs_ref[:POOL_HALO, :] = xs_ref[tm:tm + POOL_HALO, :]

    x = x_ref[...]
    inv = lax.rsqrt(jnp.mean(x * x, axis=-1, keepdims=True) + RMS_EPS)
    u = (x * inv * g_ref[...]).astype(bf16)
    for g in range(len(windows)):
        cols = slice(g * gd, (g + 1) * gd)
        xs_ref[POOL_HALO:, cols] = jnp.dot(u, win_ref[:, cols], preferred_element_type=f32)
    pos = tile_in_seq * tm + lax.broadcasted_iota(jnp.int32, (tm, 1), 0)
    for g, w in enumerate(windows):
        cols = slice(g * gd, (g + 1) * gd)
        x_self = xs_ref[POOL_HALO:POOL_HALO + tm, cols]
        s = x_self
        for d in range(1, w):
            s = s + xs_ref[POOL_HALO - d:POOL_HALO - d + tm, cols]
        count = jnp.minimum(pos + 1, w).astype(f32)
        pooled = s / count - x_self
        mixed = jnp.dot(pooled.astype(bf16), wg_ref[g], preferred_element_type=f32)
        z = jnp.dot(u, win_ref[:, di + g * gd:di + (g + 1) * gd], preferred_element_type=f32)
        y = mixed * sc_ref[:, cols] * (z * jax.nn.sigmoid(z))
        y_ref[:, cols] = y.astype(bf16)
    o_ref[...] = x_ref[...] + jnp.dot(y_ref[...], wout_ref[...], preferred_element_type=f32)


def _pool_layer(h, g, w_in, wg, scale, w_out, *, tm, seq):
    n_rows, d = h.shape
    n_groups, gd, _ = wg.shape
    di = n_groups * gd
    assert n_groups == len(POOL_WINDOWS) and w_in.shape == (d, 2 * di) and w_out.shape == (di, d)
    assert seq % tm == 0 and max(POOL_WINDOWS) <= POOL_HALO
    kern = functools.partial(_pool_layer_kernel, tm=tm, seq=seq, windows=POOL_WINDOWS, gd=gd)
    resident = dict(pipeline_mode=pl.Buffered(1))
    return pl.pallas_call(
        kern,
        grid=(n_rows // tm,),
        in_specs=[pl.BlockSpec((tm, d), lambda i: (i, 0)),
                  pl.BlockSpec((1, d), lambda i: (0, 0)),
                  pl.BlockSpec((d, 2 * di), lambda i: (0, 0), **resident),
                  pl.BlockSpec((n_groups, gd, gd), lambda i: (0, 0, 0), **resident),
                  pl.BlockSpec((1, di), lambda i: (0, 0)),
                  pl.BlockSpec((di, d), lambda i: (0, 0), **resident)],
        out_specs=pl.BlockSpec((tm, d), lambda i: (i, 0)),
        out_shape=jax.ShapeDtypeStruct((n_rows, d), f32),
        scratch_shapes=[pltpu.VMEM((tm + POOL_HALO, di), f32),
                        pltpu.VMEM((tm, di), bf16)],
        compiler_params=_cparams(1),
        name="pool_layer",
    )(h, g, w_in, wg, scale, w_out)


def _gated_out_proj_norm_kernel(o_ref, z_ref, w_ref, h_ref, g_ref, out_ref):
    z = z_ref[...].astype(f32)
    y = (o_ref[...].astype(f32) * (z * jax.nn.sigmoid(z))).astype(bf16)
    h = h_ref[...] + jnp.dot(y, w_ref[...], preferred_element_type=f32)
    inv = lax.rsqrt(jnp.mean(h * h, axis=-1, keepdims=True) + RMS_EPS)
    out_ref[...] = h * inv * g_ref[...]


def _gated_out_proj_norm(o, proj, gate_block, w, h, g, *, tm):
    n_rows, di = o.shape
    d = w.shape[1]
    return pl.pallas_call(
        _gated_out_proj_norm_kernel,
        grid=(n_rows // tm,),
        in_specs=[pl.BlockSpec((tm, di), lambda i: (i, 0)),
                  pl.BlockSpec((tm, di), lambda i: (i, gate_block)),
                  pl.BlockSpec((di, d), lambda i: (0, 0)),
                  pl.BlockSpec((tm, d), lambda i: (i, 0)),
                  pl.BlockSpec((1, d), lambda i: (0, 0))],
        out_specs=pl.BlockSpec((tm, d), lambda i: (i, 0)),
        out_shape=jax.ShapeDtypeStruct((n_rows, d), f32),
        compiler_params=_cparams(1),
        name="gated_out_proj_norm",
    )(o, proj, w, h, g)


def _suffix_sum_sublanes(a):
    sub = lax.broadcasted_iota(jnp.int32, a.shape, 0)
    t = a
    for sh in (1, 2, 4):
        rolled = pltpu.roll(t, SUBLANES - sh, axis=0)
        t = t + jnp.where(sub + sh < SUBLANES, rolled, 0.0)
    return t


def _sb_attention_kernel(q_ref, k_ref, v_ref, o_ref, kp_ref, vt_ref, *scratch):
    tq = ATT_BLOCK
    dk2 = q_ref.shape[1]
    dv = v_ref.shape[1] // 2
    n_blocks = k_ref.shape[0] // ATT_BLOCK
    heads = range(2)
    pairs = [scratch[n:n + 2] for n in range(0, len(scratch), 2)]
    zd_refs, zo0_ref, zo1_ref, p_ref, at_ref, acc_ref = (pairs[0:2], pairs[2], pairs[3], pairs[4], pairs[5],
                                                         pairs[6:8])

    p_idx = lax.broadcasted_iota(jnp.int32, (ATT_BLOCK, ATT_BLOCK), 0)
    s_idx = lax.broadcasted_iota(jnp.int32, (ATT_BLOCK, ATT_BLOCK), 1)
    perm = (s_idx == (p_idx % SUBLANES) * ATT_ROWS + p_idx // SUBLANES).astype(bf16)

    def prepare(kb):
        blk = slice(kb * ATT_BLOCK, (kb + 1) * ATT_BLOCK)
        kp_ref[blk, :] = jnp.dot(perm, k_ref[blk, :], preferred_element_type=f32).astype(bf16)
        vt_ref[kb] = jnp.dot(perm, v_ref[blk, :], preferred_element_type=f32).astype(bf16).T

    lane = lax.broadcasted_iota(jnp.int32, (tq, dk2), 1)
    c = lax.broadcasted_iota(jnp.int32, (SUBLANES, tq), 1)
    i = lax.broadcasted_iota(jnp.int32, (SUBLANES, tq), 0)
    diag_margin = c - i * ATT_ROWS

    def masked_queries(qi):
        q2 = q_ref[qi * tq:(qi + 1) * tq, :]
        zero = jnp.zeros_like(q2)
        return jnp.where(lane < dk2 // 2, q2, zero), jnp.where(lane >= dk2 // 2, q2, zero)

    def scores(z_ref, qm, kb):
        start = kb * ATT_BLOCK if isinstance(kb, int) else pl.multiple_of(kb * ATT_BLOCK, ATT_BLOCK)
        k_blk = kp_ref[pl.ds(start, ATT_BLOCK), :]
        for h in heads:
            z_ref[h][...] = lax.dot_general(k_blk, qm[h], (((1,), (1,)), ((), ())),
                                            preferred_element_type=f32)

    def pass1(z_ref, h, masked):
        acc = jnp.zeros((SUBLANES, tq), f32)
        for r in reversed(range(ATT_ROWS)):
            rows = slice(r * SUBLANES, (r + 1) * SUBLANES)
            x = z_ref[h][rows, :]
            if masked:
                x = jnp.where(diag_margin > r, x, MASKED_SCORE)
            e = jnp.exp2(-jnp.abs(x))
            lb = jnp.minimum(x, 0.0) - jnp.log(1.0 + e) * LN2_INV
            p = lb + acc
            acc = acc + (lb - x) if masked else p - x
            p_ref[h][rows, :] = p
        return acc

    def pass2(h, acc, carry):
        suffix = _suffix_sum_sublanes(acc)
        offs = carry + (suffix - acc)
        pieces = [jnp.exp2(p_ref[h][r * SUBLANES:(r + 1) * SUBLANES, :] + offs) for r in range(ATT_ROWS)]
        at_ref[h][...] = jnp.concatenate(pieces, axis=0).astype(bf16)
        return carry + jnp.broadcast_to(suffix[0:1, :], suffix.shape)

    def weighted_sum(slot, kb, h):
        acc_ref[slot][h][...] += jnp.dot(vt_ref[kb, h * dv:(h + 1) * dv, :], at_ref[h][...],
                                         preferred_element_type=f32)

    def block(z_ref, carry, masked):
        acc = [pass1(z_ref, h, masked) for h in heads]
        return tuple(pass2(h, acc[h], carry[h]) for h in heads)

    zero_carry = jnp.zeros((SUBLANES, tq), f32)
    for kb in range(min(2, n_blocks)):
        prepare(kb)
    qm_next = masked_queries(0)
    scores(zd_refs[0], qm_next, 0)
    for qi in range(n_blocks):
        slot = qi % 2
        qm = qm_next
        if qi + 2 < n_blocks:
            prepare(qi + 2)
        if qi > 0:
            scores(zo0_ref, qm, qi - 1)
        if qi + 1 < n_blocks:
            qm_next = masked_queries(qi + 1)
            scores(zd_refs[1 - slot], qm_next, qi + 1)
        if qi > 1:
            scores(zo1_ref, qm, qi - 2)
        for h in heads:
            acc_ref[slot][h][...] = jnp.zeros(acc_ref[slot][h].shape, f32)
        carry = block(zd_refs[slot], (zero_carry, zero_carry), True)
        n_done = 0
        if qi > 0:
            for h in heads:
                weighted_sum(slot, qi, h)
            carry = block(zo0_ref, carry, False)
            n_done = 1
        if qi > 1:
            z_loop = zo1_ref

            def live(cr):
                return jnp.max(jnp.maximum(cr[0], cr[1])) > -UNDERFLOW_LOG2

            def cond(state):
                return jnp.logical_and(state[0] < qi, state[1])

            def body(state):
                j, _, ca, cb = state
                kb = qi - 1 - j
                for h in heads:
                    weighted_sum(slot, kb + 1, h)
                acc = [pass1(z_loop, h, False) for h in heads]
                scores(z_loop, qm, jnp.maximum(kb - 1, 0))
                cr = tuple(pass2(h, acc[h], c_h) for h, c_h in zip(heads, (ca, cb)))
                return j + 1, live(cr), cr[0], cr[1]

            n_done = lax.while_loop(cond, body, (jnp.int32(1), live(carry), carry[0], carry[1]))[0]
        rows = slice(qi * tq, (qi + 1) * tq)
        for h in heads:
            weighted_sum(slot, qi - n_done, h)
            o_ref[rows, h * dv:(h + 1) * dv] = acc_ref[slot][h][...].T.astype(o_ref.dtype)


def _sb_attention(proj, *, batch, seq, qk_width, d_inner):
    n_rows = proj.shape[0]
    n_pairs = SB_HEADS // 2
    dk2 = 2 * SB_QK_DIM
    dv2 = 2 * (d_inner // SB_HEADS)
    k_col0 = qk_width // dk2
    v_col0 = 2 * qk_width // dv2
    assert seq % ATT_BLOCK == 0 and qk_width == SB_HEADS * SB_QK_DIM and (2 * qk_width) % dv2 == 0
    return pl.pallas_call(
        _sb_attention_kernel,
        grid=(batch, n_pairs),
        in_specs=[pl.BlockSpec((seq, dk2), lambda b, p: (b, p)),
                  pl.BlockSpec((seq, dk2), lambda b, p: (b, k_col0 + p)),
                  pl.BlockSpec((seq, dv2), lambda b, p: (b, v_col0 + p))],
        out_specs=pl.BlockSpec((seq, dv2), lambda b, p: (b, p)),
        out_shape=jax.ShapeDtypeStruct((n_rows, d_inner), bf16),
        scratch_shapes=([pltpu.VMEM((seq, dk2), bf16), pltpu.VMEM((seq // ATT_BLOCK, dv2, ATT_BLOCK), bf16)]
                        + [pltpu.VMEM((ATT_BLOCK, ATT_BLOCK), f32)] * 10
                        + [pltpu.VMEM((ATT_BLOCK, ATT_BLOCK), bf16)] * 2
                        + [pltpu.VMEM((dv2 // 2, ATT_BLOCK), f32)] * 4),
        compiler_params=_cparams(2),
        name="sb_attention",
    )(proj, proj, proj)


def kernel(x, norm_g, pool_w_in, pool_w, pool_scale, pool_w_out, sb_w_in, sb_w_out, norm_f):
    batch, seq, d = x.shape
    assert norm_g.shape[0] == 2 and pool_w_in.shape[0] == 1 and sb_w_in.shape[0] == 1
    d_inner = pool_w_out.shape[1]
    qk_width = (sb_w_in.shape[2] - 2 * d_inner) // 2
    n_rows = batch * seq
    tm = 512 if seq % 512 == 0 else ATT_BLOCK

    h = x.reshape(n_rows, d)
    h = _pool_layer(h, norm_g[0].reshape(1, d), pool_w_in[0].astype(bf16), pool_w[0].astype(bf16),
                    pool_scale[0].reshape(1, d_inner), pool_w_out[0].astype(bf16), tm=tm, seq=seq)

    proj = _norm_proj(h, norm_g[1].reshape(1, d), sb_w_in[0].astype(bf16), tm=tm, chunk=512,
                      scaled_cols=qk_width, col_scale=LOG2E / math.sqrt(SB_QK_DIM))
    o = _sb_attention(proj, batch=batch, seq=seq, qk_width=qk_width, d_inner=d_inner)
    gate_block = 2 * qk_width // d_inner + 1
    out = _gated_out_proj_norm(o, proj, gate_block, sb_w_out[0].astype(bf16), h,
                               norm_f.reshape(1, d), tm=tm)
    return out.reshape(batch, seq, d)
```

```python
import functools
import math

import jax
import jax.numpy as jnp
from jax import lax
from jax.experimental import pallas as pl
from jax.experimental.pallas import tpu as pltpu

RMS_EPS = 1e-6
POOL_WINDOWS = (2, 4, 8, 16)
POOL_HALO = 16
SB_HEADS = 16
SB_QK_DIM = 64
LOG2E = 1.4426950408889634
LN2_INV = LOG2E
MASKED_SCORE = -1e30
UNDERFLOW_LOG2 = 160.0

SUBLANES = 8
ATT_BLOCK = 256
ATT_ROWS = ATT_BLOCK // SUBLANES
VMEM_LIMIT = 56 * 1024 * 1024
POOL_LAYER_ROWS = 512
PROJ_ROWS = 1024

f32 = jnp.float32
bf16 = jnp.bfloat16


def _row_tile(n_rows, preferred):
    tile = preferred
    while n_rows % tile and tile > ATT_BLOCK:
        tile //= 2
    assert n_rows % tile == 0
    return tile


def _cparams(n_axes):
    return pltpu.CompilerParams(dimension_semantics=("arbitrary",) * n_axes,
                                vmem_limit_bytes=VMEM_LIMIT)


def _norm_proj_kernel(x_ref, g_ref, w_ref, o_ref, *, chunk, scaled_cols, col_scale):
    x = x_ref[...]
    inv = lax.rsqrt(jnp.mean(x * x, axis=-1, keepdims=True) + RMS_EPS)
    u = (x * inv * g_ref[...]).astype(bf16)
    n = o_ref.shape[1]
    for c in range(n // chunk):
        cols = slice(c * chunk, (c + 1) * chunk)
        r = jnp.dot(u, w_ref[:, cols], preferred_element_type=f32)
        if (c + 1) * chunk <= scaled_cols:
            r = r * col_scale
        o_ref[:, cols] = r.astype(o_ref.dtype)


def _norm_proj(h, g, w, *, tm, chunk, scaled_cols=0, col_scale=1.0):
    n_rows, d = h.shape
    n = w.shape[1]
    assert n_rows % tm == 0 and n % chunk == 0 and scaled_cols % chunk == 0
    kern = functools.partial(_norm_proj_kernel, chunk=chunk, scaled_cols=scaled_cols, col_scale=col_scale)
    return pl.pallas_call(
        kern,
        grid=(n_rows // tm,),
        in_specs=[pl.BlockSpec((tm, d), lambda i: (i, 0)),
                  pl.BlockSpec((1, d), lambda i: (0, 0)),
                  pl.BlockSpec((d, n), lambda i: (0, 0), pipeline_mode=pl.Buffered(1))],
        out_specs=pl.BlockSpec((tm, n), lambda i: (i, 0)),
        out_shape=jax.ShapeDtypeStruct((n_rows, n), bf16),
        compiler_params=_cparams(1),
        name="norm_proj",
    )(h, g, w)


def _pool_layer_kernel(x_ref, g_ref, win_ref, wg_ref, sc_ref, wout_ref, o_ref, xs_ref, y_ref,
                       *, tm, seq, windows, gd):
    di = gd * len(windows)
    tile_in_seq = pl.program_id(0) % (seq // tm)

    @pl.when(tile_in_seq == 0)
    def _():
        xs_ref[:POOL_HALO, :] = jnp.zeros((POOL_HALO, di), f32)

    @pl.when(tile_in_seq != 0)
    def _():
        xs_ref[:POOL_HALO, :] = xs_ref[tm:tm + POOL_HALO, :]

    x = x_ref[...]
    inv = lax.rsqrt(jnp.mean(x * x, axis=-1, keepdims=True) + RMS_EPS)
    u = (x * inv * g_ref[...]).astype(bf16)
    for g in range(len(windows)):
        cols = slice(g * gd, (g + 1) * gd)
        xs_ref[POOL_HALO:, cols] = jnp.dot(u, win_ref[:, cols], preferred_element_type=f32)
    pos = tile_in_seq * tm + lax.broadcasted_iota(jnp.int32, (tm, 1), 0)
    for g, w in enumerate(windows):
        cols = slice(g * gd, (g + 1) * gd)
        x_self = xs_ref[POOL_HALO:POOL_HALO + tm, cols]
        s = x_self
        for d in range(1, w):
            s = s + xs_ref[POOL_HALO - d:POOL_HALO - d + tm, cols]
        count = jnp.minimum(pos + 1, w).astype(f32)
        pooled = s / count - x_self
        mixed = jnp.dot(pooled.astype(bf16), wg_ref[g], preferred_element_type=f32)
        z = jnp.dot(u, win_ref[:, di + g * gd:di + (g + 1) * gd], preferred_element_type=f32)
        y = mixed * sc_ref[:, cols] * (z * jax.nn.sigmoid(z))
        y_ref[:, cols] = y.astype(bf16)
    o_ref[...] = x_ref[...] + jnp.dot(y_ref[...], wout_ref[...], preferred_element_type=f32)


def _pool_layer(h, g, w_in, wg, scale, w_out, *, tm, seq):
    n_rows, d = h.shape
    n_groups, gd, _ = wg.shape
    di = n_groups * gd
    assert n_groups == len(POOL_WINDOWS) and w_in.shape == (d, 2 * di) and w_out.shape == (di, d)
    assert seq % tm == 0 and max(POOL_WINDOWS) <= POOL_HALO
    kern = functools.partial(_pool_layer_kernel, tm=tm, seq=seq, windows=POOL_WINDOWS, gd=gd)
    resident = dict(pipeline_mode=pl.Buffered(1))
    return pl.pallas_call(
        kern,
        grid=(n_rows // tm,),
        in_specs=[pl.BlockSpec((tm, d), lambda i: (i, 0)),
                  pl.BlockSpec((1, d), lambda i: (0, 0)),
                  pl.BlockSpec((d, 2 * di), lambda i: (0, 0), **resident),
                  pl.BlockSpec((n_groups, gd, gd), lambda i: (0, 0, 0), **resident),
                  pl.BlockSpec((1, di), lambda i: (0, 0)),
                  pl.BlockSpec((di, d), lambda i: (0, 0), **resident)],
        out_specs=pl.BlockSpec((tm, d), lambda i: (i, 0)),
        out_shape=jax.ShapeDtypeStruct((n_rows, d), f32),
        scratch_shapes=[pltpu.VMEM((tm + POOL_HALO, di), f32),
                        pltpu.VMEM((tm, di), bf16)],
        compiler_params=_cparams(1),
        name="pool_layer",
    )(h, g, w_in, wg, scale, w_out)


def _gated_out_proj_norm_kernel(o_ref, z_ref, w_ref, h_ref, g_ref, out_ref):
    z = z_ref[...].astype(f32)
    y = (o_ref[...].astype(f32) * (z * jax.nn.sigmoid(z))).astype(bf16)
    h = h_ref[...] + jnp.dot(y, w_ref[...], preferred_element_type=f32)
    inv = lax.rsqrt(jnp.mean(h * h, axis=-1, keepdims=True) + RMS_EPS)
    out_ref[...] = h * inv * g_ref[...]


def _gated_out_proj_norm(o, proj, gate_block, w, h, g, *, tm):
    n_rows, di = o.shape
    d = w.shape[1]
    return pl.pallas_call(
        _gated_out_proj_norm_kernel,
        grid=(n_rows // tm,),
        in_specs=[pl.BlockSpec((tm, di), lambda i: (i, 0)),
                  pl.BlockSpec((tm, di), lambda i: (i, gate_block)),
                  pl.BlockSpec((di, d), lambda i: (0, 0), pipeline_mode=pl.Buffered(1)),
                  pl.BlockSpec((tm, d), lambda i: (i, 0)),
                  pl.BlockSpec((1, d), lambda i: (0, 0))],
        out_specs=pl.BlockSpec((tm, d), lambda i: (i, 0)),
        out_shape=jax.ShapeDtypeStruct((n_rows, d), f32),
        compiler_params=_cparams(1),
        name="gated_out_proj_norm",
    )(o, proj, w, h, g)


def _suffix_sum_sublanes(a):
    sub = lax.broadcasted_iota(jnp.int32, a.shape, 0)
    t = a
    for sh in (1, 2, 4):
        rolled = pltpu.roll(t, SUBLANES - sh, axis=0)
        t = t + jnp.where(sub + sh < SUBLANES, rolled, 0.0)
    return t


def _sb_attention_kernel(q_ref, k_ref, v_ref, o_ref, kp_ref, vt_ref, *scratch):
    tq = ATT_BLOCK
    dk2 = q_ref.shape[1]
    dv = v_ref.shape[1] // 2
    n_blocks = k_ref.shape[0] // ATT_BLOCK
    heads = range(2)
    pairs = [scratch[n:n + 2] for n in range(0, len(scratch), 2)]
    zd_refs, zo0_ref, zo1_ref, p_ref, at_ref, acc_ref = (pairs[0:2], pairs[2], pairs[3], pairs[4], pairs[5],
                                                         pairs[6:8])

    p_idx = lax.broadcasted_iota(jnp.int32, (ATT_BLOCK, ATT_BLOCK), 0)
    s_idx = lax.broadcasted_iota(jnp.int32, (ATT_BLOCK, ATT_BLOCK), 1)
    perm = (s_idx == (p_idx % SUBLANES) * ATT_ROWS + p_idx // SUBLANES).astype(bf16)

    def prepare(kb):
        blk = slice(kb * ATT_BLOCK, (kb + 1) * ATT_BLOCK)
        kp_ref[blk, :] = jnp.dot(perm, k_ref[blk, :], preferred_element_type=f32).astype(bf16)
        vt_ref[kb] = jnp.dot(perm, v_ref[blk, :], preferred_element_type=f32).astype(bf16).T

    lane = lax.broadcasted_iota(jnp.int32, (tq, dk2), 1)
    c = lax.broadcasted_iota(jnp.int32, (SUBLANES, tq), 1)
    i = lax.broadcasted_iota(jnp.int32, (SUBLANES, tq), 0)
    diag_margin = c - i * ATT_ROWS

    def masked_queries(qi):
        q2 = q_ref[qi * tq:(qi + 1) * tq, :]
        zero = jnp.zeros_like(q2)
        return jnp.where(lane < dk2 // 2, q2, zero), jnp.where(lane >= dk2 // 2, q2, zero)

    def scores(z_ref, qm, kb):
        start = kb * ATT_BLOCK if isinstance(kb, int) else pl.multiple_of(kb * ATT_BLOCK, ATT_BLOCK)
        k_blk = kp_ref[pl.ds(start, ATT_BLOCK), :]
        for h in heads:
            z_ref[h][...] = lax.dot_general(k_blk, qm[h], (((1,), (1,)), ((), ())),
                                            preferred_element_type=f32)

    def pass1(z_ref, h, masked):
        acc = jnp.zeros((SUBLANES, tq), f32)
        for r in reversed(range(ATT_ROWS)):
            rows = slice(r * SUBLANES, (r + 1) * SUBLANES)
            x = z_ref[h][rows, :]
            if masked:
                x = jnp.where(diag_margin > r, x, MASKED_SCORE)
            e = jnp.exp2(-jnp.abs(x))
            lb = jnp.minimum(x, 0.0) - jnp.log(1.0 + e) * LN2_INV
            p = lb + acc
            acc = acc + (lb - x) if masked else p - x
            p_ref[h][rows, :] = p
        return acc

    def pass2(h, acc, carry):
        suffix = _suffix_sum_sublanes(acc)
        offs = carry + (suffix - acc)
        pieces = [jnp.exp2(p_ref[h][r * SUBLANES:(r + 1) * SUBLANES, :] + offs) for r in range(ATT_ROWS)]
        at_ref[h][...] = jnp.concatenate(pieces, axis=0).astype(bf16)
        return carry + jnp.broadcast_to(suffix[0:1, :], suffix.shape)

    def weighted_sum(slot, kb, h):
        acc_ref[slot][h][...] += jnp.dot(vt_ref[kb, h * dv:(h + 1) * dv, :], at_ref[h][...],
                                         preferred_element_type=f32)

    def block(z_ref, carry, masked):
        acc = [pass1(z_ref, h, masked) for h in heads]
        return tuple(pass2(h, acc[h], carry[h]) for h in heads)

    zero_carry = jnp.zeros((SUBLANES, tq), f32)
    for kb in range(min(2, n_blocks)):
        prepare(kb)
    qm_next = masked_queries(0)
    scores(zd_refs[0], qm_next, 0)
    for qi in range(n_blocks):
        slot = qi % 2
        qm = qm_next
        if qi + 2 < n_blocks:
            prepare(qi + 2)
        if qi > 0:
            scores(zo0_ref, qm, qi - 1)
        if qi + 1 < n_blocks:
            qm_next = masked_queries(qi + 1)
            scores(zd_refs[1 - slot], qm_next, qi + 1)
        if qi > 1:
            scores(zo1_ref, qm, qi - 2)
        for h in heads:
            acc_ref[slot][h][...] = jnp.zeros(acc_ref[slot][h].shape, f32)
        carry = block(zd_refs[slot], (zero_carry, zero_carry), True)
        n_done = 0
        if qi > 0:
            for h in heads:
                weighted_sum(slot, qi, h)
            carry = block(zo0_ref, carry, False)
            n_done = 1
        if qi > 1:
            z_loop = zo1_ref

            def live(cr):
                return jnp.max(jnp.maximum(cr[0], cr[1])) > -UNDERFLOW_LOG2

            def cond(state):
                return jnp.logical_and(state[0] < qi, state[1])

            def body(state):
                j, _, ca, cb = state
                kb = qi - 1 - j
                for h in heads:
                    weighted_sum(slot, kb + 1, h)
                acc = [pass1(z_loop, h, False) for h in heads]
                scores(z_loop, qm, jnp.maximum(kb - 1, 0))
                cr = tuple(pass2(h, acc[h], c_h) for h, c_h in zip(heads, (ca, cb)))
                return j + 1, live(cr), cr[0], cr[1]

            n_done = lax.while_loop(cond, body, (jnp.int32(1), live(carry), carry[0], carry[1]))[0]
        rows = slice(qi * tq, (qi + 1) * tq)
        for h in heads:
            weighted_sum(slot, qi - n_done, h)
            o_ref[rows, h * dv:(h + 1) * dv] = acc_ref[slot][h][...].T.astype(o_ref.dtype)


def _sb_attention(proj, *, batch, seq, qk_width, d_inner):
    n_rows = proj.shape[0]
    n_pairs = SB_HEADS // 2
    dk2 = 2 * SB_QK_DIM
    dv2 = 2 * (d_inner // SB_HEADS)
    k_col0 = qk_width // dk2
    v_col0 = 2 * qk_width // dv2
    assert seq % ATT_BLOCK == 0 and qk_width == SB_HEADS * SB_QK_DIM and (2 * qk_width) % dv2 == 0
    return pl.pallas_call(
        _sb_attention_kernel,
        grid=(batch, n_pairs),
        in_specs=[pl.BlockSpec((seq, dk2), lambda b, p: (b, p)),
                  pl.BlockSpec((seq, dk2), lambda b, p: (b, k_col0 + p)),
                  pl.BlockSpec((seq, dv2), lambda b, p: (b, v_col0 + p))],
        out_specs=pl.BlockSpec((seq, dv2), lambda b, p: (b, p)),
        out_shape=jax.ShapeDtypeStruct((n_rows, d_inner), bf16),
        scratch_shapes=([pltpu.VMEM((seq, dk2), bf16), pltpu.VMEM((seq // ATT_BLOCK, dv2, ATT_BLOCK), bf16)]
                        + [pltpu.VMEM((ATT_BLOCK, ATT_BLOCK), f32)] * 10
                        + [pltpu.VMEM((ATT_BLOCK, ATT_BLOCK), bf16)] * 2
                        + [pltpu.VMEM((dv2 // 2, ATT_BLOCK), f32)] * 4),
        compiler_params=_cparams(2),
        name="sb_attention",
    )(proj, proj, proj)


def kernel(x, norm_g, pool_w_in, pool_w, pool_scale, pool_w_out, sb_w_in, sb_w_out, norm_f):
    batch, seq, d = x.shape
    assert norm_g.shape[0] == 2 and pool_w_in.shape[0] == 1 and sb_w_in.shape[0] == 1
    d_inner = pool_w_out.shape[1]
    qk_width = (sb_w_in.shape[2] - 2 * d_inner) // 2
    n_rows = batch * seq
    tm_layer = _row_tile(seq, POOL_LAYER_ROWS)
    tm_proj = _row_tile(n_rows, PROJ_ROWS)

    h = x.reshape(n_rows, d)
    h = _pool_layer(h, norm_g[0].reshape(1, d), pool_w_in[0].astype(bf16), pool_w[0].astype(bf16),
                    pool_scale[0].reshape(1, d_inner), pool_w_out[0].astype(bf16), tm=tm_layer, seq=seq)

    proj = _norm_proj(h, norm_g[1].reshape(1, d), sb_w_in[0].astype(bf16), tm=tm_proj, chunk=512,
                      scaled_cols=qk_width, col_scale=LOG2E / math.sqrt(SB_QK_DIM))
    o = _sb_attention(proj, batch=batch, seq=seq, qk_width=qk_width, d_inner=d_inner)
    gate_block = 2 * qk_width // d_inner + 1
    out = _gated_out_proj_norm(o, proj, gate_block, sb_w_out[0].astype(bf16), h,
                               norm_f.reshape(1, d), tm=tm_proj)
    return out.reshape(batch, seq, d)
```

```python
import functools
import math

import jax
import jax.numpy as jnp
from jax import lax
from jax.experimental import pallas as pl
from jax.experimental.pallas import tpu as pltpu

RMS_EPS = 1e-6
POOL_WINDOWS = (2, 4, 8, 16)
POOL_HALO = 32
SB_HEADS = 16
SB_QK_DIM = 64
LOG2E = 1.4426950408889634
LN2_INV = LOG2E
MASKED_SCORE = -1e30
UNDERFLOW_LOG2 = 160.0

SUBLANES = 8
ATT_BLOCK = 256
ATT_ROWS = ATT_BLOCK // SUBLANES
VMEM_LIMIT = 56 * 1024 * 1024
POOL_LAYER_ROWS = 512
PROJ_ROWS = 1024

f32 = jnp.float32
bf16 = jnp.bfloat16


def _row_tile(n_rows, preferred):
    tile = preferred
    while n_rows % tile and tile > ATT_BLOCK:
        tile //= 2
    assert n_rows % tile == 0
    return tile


def _cparams(n_axes):
    return pltpu.CompilerParams(dimension_semantics=("arbitrary",) * n_axes,
                                vmem_limit_bytes=VMEM_LIMIT)


def _norm_proj_kernel(x_ref, g_ref, w_ref, o_ref, *, chunk, scaled_cols, col_scale):
    x = x_ref[...]
    inv = lax.rsqrt(jnp.mean(x * x, axis=-1, keepdims=True) + RMS_EPS)
    u = (x * inv * g_ref[...]).astype(bf16)
    n = o_ref.shape[1]
    for c in range(n // chunk):
        cols = slice(c * chunk, (c + 1) * chunk)
        r = jnp.dot(u, w_ref[:, cols], preferred_element_type=f32)
        if (c + 1) * chunk <= scaled_cols:
            r = r * col_scale
        o_ref[:, cols] = r.astype(o_ref.dtype)


def _norm_proj(h, g, w, *, tm, chunk, scaled_cols=0, col_scale=1.0):
    n_rows, d = h.shape
    n = w.shape[1]
    assert n_rows % tm == 0 and n % chunk == 0 and scaled_cols % chunk == 0
    kern = functools.partial(_norm_proj_kernel, chunk=chunk, scaled_cols=scaled_cols, col_scale=col_scale)
    return pl.pallas_call(
        kern,
        grid=(n_rows // tm,),
        in_specs=[pl.BlockSpec((tm, d), lambda i: (i, 0)),
                  pl.BlockSpec((1, d), lambda i: (0, 0)),
                  pl.BlockSpec((d, n), lambda i: (0, 0), pipeline_mode=pl.Buffered(1))],
        out_specs=pl.BlockSpec((tm, n), lambda i: (i, 0)),
        out_shape=jax.ShapeDtypeStruct((n_rows, n), bf16),
        compiler_params=_cparams(1),
        name="norm_proj",
    )(h, g, w)


def _fold_weights_kernel(win_ref, wg_ref, o_ref):
    o_ref[...] = jnp.dot(win_ref[...], wg_ref[0], preferred_element_type=f32).astype(o_ref.dtype)


def _fold_group_weights(w_branch, wg):
    d, di = w_branch.shape
    n_groups, gd, _ = wg.shape
    return pl.pallas_call(
        _fold_weights_kernel,
        grid=(n_groups,),
        in_specs=[pl.BlockSpec((d, gd), lambda g: (0, g)), pl.BlockSpec((1, gd, gd), lambda g: (g, 0, 0))],
        out_specs=pl.BlockSpec((d, gd), lambda g: (0, g)),
        out_shape=jax.ShapeDtypeStruct((d, di), bf16),
        compiler_params=_cparams(1),
        name="fold_group_weights",
    )(w_branch, wg)


def _pool_layer_kernel(x_ref, g_ref, wmix_ref, wgate_ref, sc_ref, wout_ref, o_ref, *scratch,
                       tm, seq, windows, gd):
    *lvl_refs, y_ref = scratch
    rows_ext = tm + POOL_HALO
    tile_in_seq = pl.program_id(0) % (seq // tm)

    @pl.when(tile_in_seq == 0)
    def _():
        lvl_refs[0][:POOL_HALO, :] = jnp.zeros((POOL_HALO, x_ref.shape[1]), f32)

    @pl.when(tile_in_seq != 0)
    def _():
        lvl_refs[0][:POOL_HALO, :] = lvl_refs[0][tm:rows_ext, :]

    x = x_ref[...]
    inv = lax.rsqrt(jnp.mean(x * x, axis=-1, keepdims=True) + RMS_EPS)
    u = (x * inv * g_ref[...]).astype(bf16)
    lvl_refs[0][POOL_HALO:, :] = u.astype(f32)
    for k in range(1, len(lvl_refs)):
        lo, back = SUBLANES * k, 2 ** (k - 1)
        lvl_refs[k][lo:, :] = lvl_refs[k - 1][lo:, :] + lvl_refs[k - 1][lo - back:rows_ext - back, :]

    pos = tile_in_seq * tm + lax.broadcasted_iota(jnp.int32, (tm, 1), 0)
    for g, w in enumerate(windows):
        cols = slice(g * gd, (g + 1) * gd)
        count = jnp.minimum(pos + 1, w).astype(f32)
        pooled = lvl_refs[g + 1][POOL_HALO:, :] / count - lvl_refs[0][POOL_HALO:, :]
        mixed = jnp.dot(pooled.astype(bf16), wmix_ref[:, cols], preferred_element_type=f32)
        z = jnp.dot(u, wgate_ref[:, cols], preferred_element_type=f32)
        y = mixed * sc_ref[:, cols] * (z * jax.nn.sigmoid(z))
        y_ref[:, cols] = y.astype(bf16)
    o_ref[...] = x_ref[...] + jnp.dot(y_ref[...], wout_ref[...], preferred_element_type=f32)


def _pool_layer(h, g, w_mix, w_gate, scale, w_out, *, tm, seq):
    n_rows, d = h.shape
    di = w_out.shape[0]
    gd = di // len(POOL_WINDOWS)
    assert w_mix.shape == (d, di) and w_gate.shape == (d, di) and w_out.shape == (di, d)
    assert POOL_WINDOWS == tuple(2 ** (k + 1) for k in range(len(POOL_WINDOWS)))
    assert seq % tm == 0 and SUBLANES * len(POOL_WINDOWS) <= POOL_HALO
    kern = functools.partial(_pool_layer_kernel, tm=tm, seq=seq, windows=POOL_WINDOWS, gd=gd)
    resident = dict(pipeline_mode=pl.Buffered(1))
    return pl.pallas_call(
        kern,
        grid=(n_rows // tm,),
        in_specs=[pl.BlockSpec((tm, d), lambda i: (i, 0)),
                  pl.BlockSpec((1, d), lambda i: (0, 0)),
                  pl.BlockSpec((d, di), lambda i: (0, 0), **resident),
                  pl.BlockSpec((d, di), lambda i: (0, 0), **resident),
                  pl.BlockSpec((1, di), lambda i: (0, 0)),
                  pl.BlockSpec((di, d), lambda i: (0, 0), **resident)],
        out_specs=pl.BlockSpec((tm, d), lambda i: (i, 0)),
        out_shape=jax.ShapeDtypeStruct((n_rows, d), f32),
        scratch_shapes=([pltpu.VMEM((tm + POOL_HALO, d), f32)] * (len(POOL_WINDOWS) + 1)
                        + [pltpu.VMEM((tm, di), bf16)]),
        compiler_params=_cparams(1),
        name="pool_layer",
    )(h, g, w_mix, w_gate, scale, w_out)


def _gated_out_proj_norm_kernel(o_ref, z_ref, w_ref, h_ref, g_ref, out_ref):
    z = z_ref[...].astype(f32)
    y = (o_ref[...].astype(f32) * (z * jax.nn.sigmoid(z))).astype(bf16)
    h = h_ref[...] + jnp.dot(y, w_ref[...], preferred_element_type=f32)
    inv = lax.rsqrt(jnp.mean(h * h, axis=-1, keepdims=True) + RMS_EPS)
    out_ref[...] = h * inv * g_ref[...]


def _gated_out_proj_norm(o, proj, gate_block, w, h, g, *, tm):
    n_rows, di = o.shape
    d = w.shape[1]
    return pl.pallas_call(
        _gated_out_proj_norm_kernel,
        grid=(n_rows // tm,),
        in_specs=[pl.BlockSpec((tm, di), lambda i: (i, 0)),
                  pl.BlockSpec((tm, di), lambda i: (i, gate_block)),
                  pl.BlockSpec((di, d), lambda i: (0, 0), pipeline_mode=pl.Buffered(1)),
                  pl.BlockSpec((tm, d), lambda i: (i, 0)),
                  pl.BlockSpec((1, d), lambda i: (0, 0))],
        out_specs=pl.BlockSpec((tm, d), lambda i: (i, 0)),
        out_shape=jax.ShapeDtypeStruct((n_rows, d), f32),
        compiler_params=_cparams(1),
        name="gated_out_proj_norm",
    )(o, proj, w, h, g)


def _suffix_sum_sublanes(a):
    sub = lax.broadcasted_iota(jnp.int32, a.shape, 0)
    t = a
    for sh in (1, 2, 4):
        rolled = pltpu.roll(t, SUBLANES - sh, axis=0)
        t = t + jnp.where(sub + sh < SUBLANES, rolled, 0.0)
    return t


def _sb_attention_kernel(q_ref, k_ref, v_ref, o_ref, kp_ref, vt_ref, *scratch):
    tq = ATT_BLOCK
    dk2 = q_ref.shape[1]
    dv = v_ref.shape[1] // 2
    n_blocks = k_ref.shape[0] // ATT_BLOCK
    heads = range(2)
    pairs = [scratch[n:n + 2] for n in range(0, len(scratch), 2)]
    zd_refs, zo0_ref, zo1_ref, p_ref, at_ref, acc_ref = (pairs[0:2], pairs[2], pairs[3], pairs[4], pairs[5],
                                                         pairs[6:8])

    p_idx = lax.broadcasted_iota(jnp.int32, (ATT_BLOCK, ATT_BLOCK), 0)
    s_idx = lax.broadcasted_iota(jnp.int32, (ATT_BLOCK, ATT_BLOCK), 1)
    perm = (s_idx == (p_idx % SUBLANES) * ATT_ROWS + p_idx // SUBLANES).astype(bf16)

    def prepare(kb):
        blk = slice(kb * ATT_BLOCK, (kb + 1) * ATT_BLOCK)
        kp_ref[blk, :] = jnp.dot(perm, k_ref[blk, :], preferred_element_type=f32).astype(bf16)
        vt_ref[kb] = jnp.dot(perm, v_ref[blk, :], preferred_element_type=f32).astype(bf16).T

    lane = lax.broadcasted_iota(jnp.int32, (tq, dk2), 1)
    c = lax.broadcasted_iota(jnp.int32, (SUBLANES, tq), 1)
    i = lax.broadcasted_iota(jnp.int32, (SUBLANES, tq), 0)
    diag_margin = c - i * ATT_ROWS

    def masked_queries(qi):
        q2 = q_ref[qi * tq:(qi + 1) * tq, :]
        zero = jnp.zeros_like(q2)
        return jnp.where(lane < dk2 // 2, q2, zero), jnp.where(lane >= dk2 // 2, q2, zero)

    def scores(z_ref, qm, kb):
        start = kb * ATT_BLOCK if isinstance(kb, int) else pl.multiple_of(kb * ATT_BLOCK, ATT_BLOCK)
        k_blk = kp_ref[pl.ds(start, ATT_BLOCK), :]
        for h in heads:
            z_ref[h][...] = lax.dot_general(k_blk, qm[h], (((1,), (1,)), ((), ())),
                                            preferred_element_type=f32)

    def pass1(z_ref, h, masked):
        acc = jnp.zeros((SUBLANES, tq), f32)
        for r in reversed(range(ATT_ROWS)):
            rows = slice(r * SUBLANES, (r + 1) * SUBLANES)
            x = z_ref[h][rows, :]
            if masked:
                x = jnp.where(diag_margin > r, x, MASKED_SCORE)
            e = jnp.exp2(-jnp.abs(x))
            lb = jnp.minimum(x, 0.0) - jnp.log(1.0 + e) * LN2_INV
            p = lb + acc
            acc = acc + (lb - x) if masked else p - x
            p_ref[h][rows, :] = p
        return acc

    def pass2(h, acc, carry):
        suffix = _suffix_sum_sublanes(acc)
        offs = carry + (suffix - acc)
        pieces = [jnp.exp2(p_ref[h][r * SUBLANES:(r + 1) * SUBLANES, :] + offs) for r in range(ATT_ROWS)]
        at_ref[h][...] = jnp.concatenate(pieces, axis=0).astype(bf16)
        return carry + jnp.broadcast_to(suffix[0:1, :], suffix.shape)

    def weighted_sum(slot, kb, h):
        acc_ref[slot][h][...] += jnp.dot(vt_ref[kb, h * dv:(h + 1) * dv, :], at_ref[h][...],
                                         preferred_element_type=f32)

    def block(z_ref, carry, masked):
        acc = [pass1(z_ref, h, masked) for h in heads]
        return tuple(pass2(h, acc[h], carry[h]) for h in heads)

    zero_carry = jnp.zeros((SUBLANES, tq), f32)
    for kb in range(min(2, n_blocks)):
        prepare(kb)
    qm_next = masked_queries(0)
    scores(zd_refs[0], qm_next, 0)
    for qi in range(n_blocks):
        slot = qi % 2
        qm = qm_next
        if qi + 2 < n_blocks:
            prepare(qi + 2)
        if qi > 0:
            scores(zo0_ref, qm, qi - 1)
        if qi + 1 < n_blocks:
            qm_next = masked_queries(qi + 1)
            scores(zd_refs[1 - slot], qm_next, qi + 1)
        if qi > 1:
            scores(zo1_ref, qm, qi - 2)
        for h in heads:
            acc_ref[slot][h][...] = jnp.zeros(acc_ref[slot][h].shape, f32)
        carry = block(zd_refs[slot], (zero_carry, zero_carry), True)
        n_done = 0
        if qi > 0:
            for h in heads:
                weighted_sum(slot, qi, h)
            carry = block(zo0_ref, carry, False)
            n_done = 1
        if qi > 1:
            z_loop = zo1_ref

            def live(cr):
                return jnp.max(jnp.maximum(cr[0], cr[1])) > -UNDERFLOW_LOG2

            def cond(state):
                return jnp.logical_and(state[0] < qi, state[1])

            def body(state):
                j, _, ca, cb = state
                kb = qi - 1 - j
                for h in heads:
                    weighted_sum(slot, kb + 1, h)
                acc = [pass1(z_loop, h, False) for h in heads]
                scores(z_loop, qm, jnp.maximum(kb - 1, 0))
                cr = tuple(pass2(h, acc[h], c_h) for h, c_h in zip(heads, (ca, cb)))
                return j + 1, live(cr), cr[0], cr[1]

            n_done = lax.while_loop(cond, body, (jnp.int32(1), live(carry), carry[0], carry[1]))[0]
        rows = slice(qi * tq, (qi + 1) * tq)
        for h in heads:
            weighted_sum(slot, qi - n_done, h)
            o_ref[rows, h * dv:(h + 1) * dv] = acc_ref[slot][h][...].T.astype(o_ref.dtype)


def _sb_attention(proj, *, batch, seq, qk_width, d_inner):
    n_rows = proj.shape[0]
    n_pairs = SB_HEADS // 2
    dk2 = 2 * SB_QK_DIM
    dv2 = 2 * (d_inner // SB_HEADS)
    k_col0 = qk_width // dk2
    v_col0 = 2 * qk_width // dv2
    assert seq % ATT_BLOCK == 0 and qk_width == SB_HEADS * SB_QK_DIM and (2 * qk_width) % dv2 == 0
    return pl.pallas_call(
        _sb_attention_kernel,
        grid=(batch, n_pairs),
        in_specs=[pl.BlockSpec((seq, dk2), lambda b, p: (b, p)),
                  pl.BlockSpec((seq, dk2), lambda b, p: (b, k_col0 + p)),
                  pl.BlockSpec((seq, dv2), lambda b, p: (b, v_col0 + p))],
        out_specs=pl.BlockSpec((seq, dv2), lambda b, p: (b, p)),
        out_shape=jax.ShapeDtypeStruct((n_rows, d_inner), bf16),
        scratch_shapes=([pltpu.VMEM((seq, dk2), bf16), pltpu.VMEM((seq // ATT_BLOCK, dv2, ATT_BLOCK), bf16)]
                        + [pltpu.VMEM((ATT_BLOCK, ATT_BLOCK), f32)] * 10
                        + [pltpu.VMEM((ATT_BLOCK, ATT_BLOCK), bf16)] * 2
                        + [pltpu.VMEM((dv2 // 2, ATT_BLOCK), f32)] * 4),
        compiler_params=_cparams(2),
        name="sb_attention",
    )(proj, proj, proj)


def kernel(x, norm_g, pool_w_in, pool_w, pool_scale, pool_w_out, sb_w_in, sb_w_out, norm_f):
    batch, seq, d = x.shape
    assert norm_g.shape[0] == 2 and pool_w_in.shape[0] == 1 and sb_w_in.shape[0] == 1
    d_inner = pool_w_out.shape[1]
    qk_width = (sb_w_in.shape[2] - 2 * d_inner) // 2
    n_rows = batch * seq
    tm_layer = _row_tile(seq, POOL_LAYER_ROWS)
    tm_proj = _row_tile(n_rows, PROJ_ROWS)

    h = x.reshape(n_rows, d)
    w_in = pool_w_in[0].astype(bf16)
    w_mix = _fold_group_weights(w_in[:, :d_inner], pool_w[0].astype(bf16))
    h = _pool_layer(h, norm_g[0].reshape(1, d), w_mix, w_in[:, d_inner:], pool_scale[0].reshape(1, d_inner),
                    pool_w_out[0].astype(bf16), tm=tm_layer, seq=seq)

    proj = _norm_proj(h, norm_g[1].reshape(1, d), sb_w_in[0].astype(bf16), tm=tm_proj, chunk=512,
                      scaled_cols=qk_width, col_scale=LOG2E / math.sqrt(SB_QK_DIM))
    o = _sb_attention(proj, batch=batch, seq=seq, qk_width=qk_width, d_inner=d_inner)
    gate_block = 2 * qk_width // d_inner + 1
    out = _gated_out_proj_norm(o, proj, gate_block, sb_w_out[0].astype(bf16), h,
                               norm_f.reshape(1, d), tm=tm_proj)
    return out.reshape(batch, seq, d)
```
